```python
import math
import jax, jax.numpy as jnp
from jax import lax
import numpy as np

D_MODEL = 1024
BATCH = 8
SEQ = 2048
DEPTH = 2
DEC_BATCH = 128
DEC_SEQ = 4
PAST_LEN = 2048
PAGE_SIZE = 128

N_HEADS = 16
HEAD_DIM = D_MODEL // N_HEADS
KV_HEADS = 4
HPG = N_HEADS // KV_HEADS
CMP_BLOCK = 32
CMP_STRIDE = 16
CMP_HIDDEN = HEAD_DIM
SEL_BLOCK = 64
N_SEL = 16
WINDOW = 512
Q_BLOCK = 128
NSA_IN_COLS = N_HEADS * HEAD_DIM + 3 * 2 * KV_HEADS * HEAD_DIM + 3 * N_HEADS
RWKV_HEAD = 64
RWKV_HEADS = D_MODEL // RWKV_HEAD
DECAY_LORA = 64
AAA_LORA = 64
GATE_LORA = 128
GN_EPS = 64e-5
N_EXPERTS = 32
TOP_K = 4
D_EXPERT = D_MODEL
SWIGLU_ALPHA = 1.702
SWIGLU_LIMIT = 7.0
MOE_BLOCK = 128
LN_EPS = 1e-5
DN_ALPHA = (2 * DEPTH) ** 0.25
DN_BETA = (8 * DEPTH) ** -0.25
N_NSA_LAYERS = (DEPTH + 1) // 2
N_RWKV_LAYERS = DEPTH // 2

kernel_name = 'nsa_rwkv7_moe_hybrid_step'


def layer_norm(x, g, b):
    xf = x.astype(jnp.float32)
    mu = jnp.mean(xf, -1, keepdims=True)
    var = jnp.mean(jnp.square(xf - mu), -1, keepdims=True)
    return ((xf - mu) * lax.rsqrt(var + LN_EPS) * g + b).astype(x.dtype)


def masked_softmax(s, mask, axis):
    s = jnp.where(mask, s.astype(jnp.float32), -jnp.inf)
    m = jnp.max(s, axis=axis, keepdims=True)
    m = jnp.where(jnp.isfinite(m), m, 0.0)
    e = jnp.where(mask, jnp.exp(s - m), 0.0)
    return e / jnp.maximum(jnp.sum(e, axis=axis, keepdims=True), 1e-30)


def cmp_sel_overlap(n_cmp, n_blk):
    c0 = np.arange(n_cmp)[:, None] * CMP_STRIDE
    c1 = c0 + CMP_BLOCK - 1
    s0 = np.arange(n_blk)[None, :] * SEL_BLOCK
    s1 = s0 + SEL_BLOCK - 1
    return ((c0 <= s1) & (c1 >= s0)).astype(np.float32)


def nsa_project(x, w_in):
    n, t, _ = x.shape
    h = x @ w_in
    qd = N_HEADS * HEAD_DIM
    kvd = 3 * 2 * KV_HEADS * HEAD_DIM
    q = h[..., :qd].reshape(n, t, KV_HEADS, HPG, HEAD_DIM)
    kv = h[..., qd:qd + kvd].reshape(n, t, 3, 2, KV_HEADS, HEAD_DIM)
    gates = jax.nn.sigmoid(h[..., qd + kvd:].reshape(n, t, KV_HEADS, HPG, 3))
    return q, kv, gates


def compress(kv, pe, w1, b1, w2, b2):
    n, length = kv.shape[:2]
    n_cmp = (length - CMP_BLOCK) // CMP_STRIDE + 1
    idx = np.arange(n_cmp)[:, None] * CMP_STRIDE + np.arange(CMP_BLOCK)[None, :]
    blk = kv[:, idx] + pe[:, None, :]
    blk = jnp.swapaxes(blk, 2, 3).reshape(n, n_cmp, KV_HEADS, CMP_BLOCK * HEAD_DIM)
    return jax.nn.gelu(blk @ w1 + b1) @ w2 + b2


def sel_blocks(k):
    n, length = k.shape[:2]
    n_blk = -(-length // SEL_BLOCK)
    k = jnp.pad(k, ((0, 0), (0, n_blk * SEL_BLOCK - length), (0, 0), (0, 0)))
    return k.reshape(n, n_blk, SEL_BLOCK, KV_HEADS, HEAD_DIM).transpose(0, 3, 1, 2, 4)


def branch_keys(rows, cmp_pe, cmp_w1, cmp_b1, cmp_w2, cmp_b2):
    kc = compress(rows[:, :, 0], cmp_pe[0], cmp_w1[0], cmp_b1[0], cmp_w2[0], cmp_b2[0])
    vc = compress(rows[:, :, 1], cmp_pe[1], cmp_w1[1], cmp_b1[1], cmp_w2[1], cmp_b2[1])
    cmp_end = jnp.arange(kc.shape[1]) * CMP_STRIDE + (CMP_BLOCK - 1)
    return kc, vc, cmp_end, sel_blocks(rows[:, :, 2]), sel_blocks(rows[:, :, 3])


def nsa_attend(q, qpos, kc, vc, cmp_end, ks, vs, kw, vw, wpos, gates):
    f32 = jnp.float32
    scale = HEAD_DIM ** -0.5
    s_c = jnp.einsum('ntghd,ncgd->ntghc', q, kc, preferred_element_type=f32) * scale
    m_c = (cmp_end[None, :] <= qpos[:, None])[None, :, None, None, :]
    p_c = masked_softmax(s_c, m_c, -1)
    o_c = jnp.einsum('ntghc,ncgd->ntghd', p_c.astype(vc.dtype), vc)
    n_cmp, n_blk = kc.shape[1], ks.shape[2]
    overlap = jnp.asarray(cmp_sel_overlap(n_cmp, n_blk))
    imp = jnp.einsum('ntghc,cs->ntgs', p_c, overlap)
    qblk = (qpos // SEL_BLOCK)[:, None]
    blk = jnp.arange(n_blk)[None, :]
    forced = (blk == 0) | (blk == qblk) | (blk == qblk - 1)
    imp = jnp.where(forced[None, :, None, :], jnp.inf, imp)
    imp = jnp.where((blk <= qblk)[None, :, None, :], imp, -jnp.inf)
    top_s, top_i = lax.top_k(imp, min(N_SEL, n_blk))
    n_idx = jnp.arange(q.shape[0])[:, None, None, None]
    g_idx = jnp.arange(KV_HEADS)[None, None, :, None]
    k_sel = ks[n_idx, g_idx, top_i]
    v_sel = vs[n_idx, g_idx, top_i]
    s_s = jnp.einsum('ntghd,ntgjbd->ntghjb', q, k_sel, preferred_element_type=f32) * scale
    kpos = top_i[..., None] * SEL_BLOCK + jnp.arange(SEL_BLOCK)
    m_s = (kpos <= qpos[None, :, None, None, None]) & (top_s > -jnp.inf)[..., None]
    p_s = masked_softmax(s_s, m_s[:, :, :, None], (-2, -1))
    o_s = jnp.einsum('ntghjb,ntgjbd->ntghd', p_s.astype(v_sel.dtype), v_sel)
    s_w = jnp.einsum('ntghd,nlgd->ntghl', q, kw, preferred_element_type=f32) * scale
    m_w = ((wpos[None, :] <= qpos[:, None]) & (wpos[None, :] > qpos[:, None] - WINDOW)
           & (wpos[None, :] >= 0))
    p_w = masked_softmax(s_w, m_w[None, :, None, None, :], -1)
    o_w = jnp.einsum('ntghl,nlgd->ntghd', p_w.astype(vw.dtype), vw)
    return gates[..., 0:1] * o_c + gates[..., 1:2] * o_s + gates[..., 2:3] * o_w


def nsa_prompt(x, w_in, w_out, cmp_pe, cmp_w1, cmp_b1, cmp_w2, cmp_b2):
    b, t, _ = x.shape
    q, kv, gates = nsa_project(x, w_in)
    rows = kv[:, :, :2].reshape(b, t, 4, KV_HEADS, HEAD_DIM)
    win = kv[:, :, 2]
    kc, vc, cmp_end, ks, vs = branch_keys(rows, cmp_pe, cmp_w1, cmp_b1, cmp_w2, cmp_b2)
    win_pad = jnp.pad(win, ((0, 0), (WINDOW, 0), (0, 0), (0, 0), (0, 0)))

    def one_block(c):
        s = c * Q_BLOCK
        qb = lax.dynamic_slice_in_dim(q, s, Q_BLOCK, 1)
        gb = lax.dynamic_slice_in_dim(gates, s, Q_BLOCK, 1)
        wb = lax.dynamic_slice_in_dim(win_pad, s, WINDOW + Q_BLOCK, 1)
        qpos = s + jnp.arange(Q_BLOCK)
        wpos = s - WINDOW + jnp.arange(WINDOW + Q_BLOCK)
        return nsa_attend(qb, qpos, kc, vc, cmp_end, ks, vs, wb[:, :, 0], wb[:, :, 1], wpos, gb)

    o = lax.map(one_block, jnp.arange(t // Q_BLOCK))
    o = jnp.moveaxis(o, 0, 1).reshape(b, t, N_HEADS * HEAD_DIM)
    keep = min(WINDOW, t)
    return o @ w_out, rows, win[:, t - keep:]


def nsa_sample(x, past_rows, win_buf, w_in, w_out, cmp_pe, cmp_w1, cmp_b1, cmp_w2, cmp_b2):
    db, ds, _ = x.shape
    past_len = past_rows.shape[1]
    q, kv, gates = nsa_project(x, w_in)
    new_rows = kv[:, :, :2].reshape(db, ds, 4, KV_HEADS, HEAD_DIM)
    rows = jnp.concatenate([past_rows, new_rows], axis=1)
    kc, vc, cmp_end, ks, vs = branch_keys(rows, cmp_pe, cmp_w1, cmp_b1, cmp_w2, cmp_b2)
    wc = win_buf.shape[1]
    win_all = jnp.concatenate([win_buf, kv[:, :, 2]], axis=1)
    wpos = past_len - wc + jnp.arange(wc + ds)
    qpos = past_len + jnp.arange(ds)
    o = nsa_attend(q, qpos, kc, vc, cmp_end, ks, vs, win_all[:, :, 0], win_all[:, :, 1], wpos, gates)
    y = o.reshape(db, ds, N_HEADS * HEAD_DIM) @ w_out
    return y, new_rows, win_all[:, ds:]


def wkv_scan(s0, r, decay, k, v, a_vec, b_vec):
    def step(s, inp):
        r_t, w_t, k_t, v_t, a_t, b_t = inp
        sa = jnp.einsum('nhij,nhj->nhi', s, a_t)
        s = s * w_t[:, :, None, :] + sa[..., None] * b_t[:, :, None, :] + v_t[..., None] * k_t[:, :, None, :]
        return s, jnp.einsum('nhij,nhj->nhi', s, r_t)
    xs = tuple(jnp.moveaxis(z, 1, 0) for z in (r, decay, k, v, a_vec, b_vec))
    s, ys = lax.scan(step, s0.astype(jnp.float32), xs)
    return s, jnp.moveaxis(ys, 0, 1)


def rwkv_mix(x, shift_prev, s0, mu, w_rkv, w0, w1, w2, a0, a1, a2, g1, g2,
             k_k, k_a, r_k, gn_g, gn_b, w_out):
    n, t, d = x.shape
    x_prev = jnp.concatenate([shift_prev[:, None].astype(x.dtype), x[:, :-1]], axis=1)
    xm = x[None] + (x_prev - x)[None] * mu[:, None, None, :]
    rkv = jnp.einsum('sntd,sde->snte', xm[:3], w_rkv)
    r, k, v = rkv[0], rkv[1], rkv[2]
    logw = -jax.nn.softplus(-(w0 + jnp.tanh(xm[3] @ w1) @ w2)) - 0.5
    a = jax.nn.sigmoid(a0 + (xm[4] @ a1) @ a2)
    g = jax.nn.sigmoid(xm[5] @ g1) @ g2
    hs = lambda z: z.reshape(n, t, RWKV_HEADS, RWKV_HEAD).astype(jnp.float32)
    kk = hs(k * k_k)
    kk = kk / jnp.maximum(jnp.sqrt(jnp.sum(kk * kk, -1, keepdims=True)), 1e-12)
    kh = hs(k * (1 + (a - 1) * k_a))
    rh, vh, ah = hs(r), hs(v), hs(a)
    decay = jnp.exp(-jnp.exp(hs(logw)))
    s, y = wkv_scan(s0, rh, decay, kh, vh, -kk, kk * ah)
    mu_y = jnp.mean(y, -1, keepdims=True)
    var_y = jnp.mean(jnp.square(y - mu_y), -1, keepdims=True)
    yn = ((y - mu_y) * lax.rsqrt(var_y + GN_EPS)).reshape(n, t, d) * gn_g + gn_b
    bonus = (jnp.sum(rh * kh * r_k, -1, keepdims=True) * vh).reshape(n, t, d)
    out = ((yn + bonus).astype(x.dtype) * g) @ w_out
    return out, s.astype(s0.dtype), x[:, -1]


def moe(h, w_router, b_router, w_up, b_up, w_down, b_down):
    shp = h.shape
    x = h.reshape(-1, D_MODEL)
    n = x.shape[0]
    logits = jnp.dot(x, w_router, preferred_element_type=jnp.float32) + b_router
    top_v, top_e = lax.top_k(logits, TOP_K)
    gate = jax.nn.softmax(top_v, axis=-1)
    n_assign = n * TOP_K
    flat_e = top_e.reshape(-1)
    order = jnp.argsort(flat_e)
    e_sorted = flat_e[order]
    tok = order // TOP_K
    counts = jnp.zeros((N_EXPERTS,), jnp.int32).at[flat_e].add(1)
    padded = (counts + MOE_BLOCK - 1) // MOE_BLOCK * MOE_BLOCK
    pad_end = jnp.cumsum(padded)
    pad_start = pad_end - padded
    raw_start = jnp.cumsum(counts) - counts
    dest = pad_start[e_sorted] + jnp.arange(n_assign) - raw_start[e_sorted]
    n_blocks = -(-n_assign // MOE_BLOCK) + N_EXPERTS
    buf = jnp.zeros((n_blocks * MOE_BLOCK, D_MODEL), x.dtype).at[dest].set(x[tok])
    block_start = jnp.arange(n_blocks) * MOE_BLOCK
    block_e = jnp.minimum(jnp.sum(pad_end[None, :] <= block_start[:, None], axis=1), N_EXPERTS - 1)

    def expert_block(args):
        xb, e = args
        u = xb @ w_up[e] + b_up[e]
        glu = jnp.minimum(u[:, :D_EXPERT], SWIGLU_LIMIT)
        lin = jnp.clip(u[:, D_EXPERT:], -SWIGLU_LIMIT, SWIGLU_LIMIT)
        return (glu * jax.nn.sigmoid(SWIGLU_ALPHA * glu) * (lin + 1)) @ w_down[e] + b_down[e]

    yb = lax.map(expert_block, (buf.reshape(n_blocks, MOE_BLOCK, D_MODEL), block_e))
    y_sorted = yb.reshape(-1, D_MODEL)[dest]
    w_sorted = gate.reshape(-1)[order].astype(y_sorted.dtype)
    y = jnp.zeros_like(x).at[tok].add(y_sorted * w_sorted[:, None])
    return y.reshape(shp)


def setup_inputs(seed: int = 0) -> dict:
    key = jax.random.key(seed)
    keys = iter(jax.random.split(key, 48))

    def nrm(shape, scale=1.0):
        return jax.random.normal(next(keys), shape, jnp.float32) * scale

    D = D_MODEL
    LA, LB = N_NSA_LAYERS, N_RWKV_LAYERS
    n_pages = PAST_LEN // PAGE_SIZE
    n_used = DEC_BATCH * n_pages
    n_phys = n_used + max(1, n_used // 4)
    win_cache = min(WINDOW, PAST_LEN)
    perm = jax.random.permutation(next(keys), n_phys)
    page_table = perm[:n_used].reshape(DEC_BATCH, n_pages).astype(jnp.int32)
    return {
        'x_prompt': nrm((BATCH, SEQ, D)),
        'x_sample': nrm((DEC_BATCH, DEC_SEQ, D)),
        'cache_nsa_paged': nrm((n_phys, PAGE_SIZE, LA, 4, KV_HEADS, HEAD_DIM)),
        'cache_nsa_win': nrm((LA, DEC_BATCH, win_cache, 2, KV_HEADS, HEAD_DIM)),
        'state_rwkv_wkv': nrm((LB, DEC_BATCH, RWKV_HEADS, RWKV_HEAD, RWKV_HEAD), 0.5),
        'state_rwkv_shift': nrm((LB, DEC_BATCH, D)),
        'page_table': page_table,
        'nsa_w_in': nrm((LA, D, NSA_IN_COLS), D ** -0.5),
        'nsa_w_out': nrm((LA, N_HEADS * HEAD_DIM, D), (N_HEADS * HEAD_DIM) ** -0.5 * DN_BETA),
        'nsa_cmp_pe': nrm((LA, 2, CMP_BLOCK, HEAD_DIM), 0.5),
        'nsa_cmp_w1': nrm((LA, 2, CMP_BLOCK * HEAD_DIM, CMP_HIDDEN), (CMP_BLOCK * HEAD_DIM) ** -0.5),
        'nsa_cmp_b1': nrm((LA, 2, CMP_HIDDEN), 0.02),
        'nsa_cmp_w2': nrm((LA, 2, CMP_HIDDEN, HEAD_DIM), CMP_HIDDEN ** -0.5),
        'nsa_cmp_b2': nrm((LA, 2, HEAD_DIM), 0.02),
        'rwkv_mu': jax.random.uniform(next(keys), (LB, 6, D), jnp.float32),
        'rwkv_w_rkv': nrm((LB, 3, D, D), D ** -0.5),
        'rwkv_w0': jnp.linspace(-6.5, -1.5, D, dtype=jnp.float32)[None, :] + nrm((LB, D), 0.1),
        'rwkv_w1': nrm((LB, D, DECAY_LORA), D ** -0.5),
        'rwkv_w2': nrm((LB, DECAY_LORA, D), 0.1),
        'rwkv_a0': nrm((LB, D), 0.1),
        'rwkv_a1': nrm((LB, D, AAA_LORA), D ** -0.5),
        'rwkv_a2': nrm((LB, AAA_LORA, D), 0.1),
        'rwkv_g1': nrm((LB, D, GATE_LORA), D ** -0.5),
        'rwkv_g2': nrm((LB, GATE_LORA, D), GATE_LORA ** -0.5),
        'rwkv_k_k': 0.85 + nrm((LB, D), 0.05),
        'rwkv_k_a': 1.0 + nrm((LB, D), 0.05),
        'rwkv_r_k': nrm((LB, RWKV_HEADS, RWKV_HEAD), 0.1),
        'rwkv_gn_g': 1.0 + nrm((LB, D), 0.05),
        'rwkv_gn_b': nrm((LB, D), 0.02),
        'rwkv_w_out': nrm((LB, D, D), D ** -0.5 * DN_BETA),
        'ln_g': 1.0 + nrm((DEPTH, 2, D), 0.05),
        'ln_b': nrm((DEPTH, 2, D), 0.02),
        'moe_w_router': nrm((DEPTH, D, N_EXPERTS), D ** -0.5),
        'moe_b_router': nrm((DEPTH, N_EXPERTS), 0.01),
        'moe_w_up': nrm((DEPTH, N_EXPERTS, D, 2 * D_EXPERT), D ** -0.5),
        'moe_b_up': nrm((DEPTH, N_EXPERTS, 2 * D_EXPERT), 0.02),
        'moe_w_down': nrm((DEPTH, N_EXPERTS, D_EXPERT, D), D_EXPERT ** -0.5 * DN_BETA),
        'moe_b_down': nrm((DEPTH, N_EXPERTS, D), 0.02),
    }


def reference(x_prompt, x_sample, cache_nsa_paged, cache_nsa_win, state_rwkv_wkv, state_rwkv_shift,
              page_table, nsa_w_in, nsa_w_out, nsa_cmp_pe, nsa_cmp_w1, nsa_cmp_b1, nsa_cmp_w2, nsa_cmp_b2,
              rwkv_mu, rwkv_w_rkv, rwkv_w0, rwkv_w1, rwkv_w2, rwkv_a0, rwkv_a1, rwkv_a2, rwkv_g1, rwkv_g2,
              rwkv_k_k, rwkv_k_a, rwkv_r_k, rwkv_gn_g, rwkv_gn_b, rwkv_w_out,
              ln_g, ln_b, moe_w_router, moe_b_router, moe_w_up, moe_b_up, moe_w_down, moe_b_down):
    xp, xs = x_prompt, x_sample
    past = cache_nsa_paged[page_table]
    past = past.reshape(past.shape[0], past.shape[1] * past.shape[2], *past.shape[3:])
    p_rows, p_win, s_rows, s_win = [], [], [], []
    p_wkv, p_shift, s_wkv, s_shift = [], [], [], []
    for i in range(DEPTH):
        j = i // 2
        if i % 2 == 0:
            cmp = (nsa_cmp_pe[j], nsa_cmp_w1[j], nsa_cmp_b1[j], nsa_cmp_w2[j], nsa_cmp_b2[j])
            yp, rows_p, win_p = nsa_prompt(xp, nsa_w_in[j], nsa_w_out[j], *cmp)
            ys, rows_s, win_s = nsa_sample(xs, past[:, :, j], cache_nsa_win[j], nsa_w_in[j], nsa_w_out[j], *cmp)
            p_rows.append(rows_p); p_win.append(win_p); s_rows.append(rows_s); s_win.append(win_s)
        else:
            rw = (rwkv_mu[j], rwkv_w_rkv[j], rwkv_w0[j], rwkv_w1[j], rwkv_w2[j], rwkv_a0[j], rwkv_a1[j],
                  rwkv_a2[j], rwkv_g1[j], rwkv_g2[j], rwkv_k_k[j], rwkv_k_a[j], rwkv_r_k[j],
                  rwkv_gn_g[j], rwkv_gn_b[j], rwkv_w_out[j])
            b = xp.shape[0]
            zero_shift = jnp.zeros((b, D_MODEL), xp.dtype)
            zero_state = jnp.zeros((b, RWKV_HEADS, RWKV_HEAD, RWKV_HEAD), xp.dtype)
            yp, wkv_p, sh_p = rwkv_mix(xp, zero_shift, zero_state, *rw)
            ys, wkv_s, sh_s = rwkv_mix(xs, state_rwkv_shift[j], state_rwkv_wkv[j], *rw)
            p_wkv.append(wkv_p); p_shift.append(sh_p); s_wkv.append(wkv_s); s_shift.append(sh_s)
        xp = layer_norm(DN_ALPHA * xp + yp, ln_g[i, 0], ln_b[i, 0])
        xs = layer_norm(DN_ALPHA * xs + ys, ln_g[i, 0], ln_b[i, 0])
        ffn = (moe_w_router[i], moe_b_router[i], moe_w_up[i], moe_b_up[i], moe_w_down[i], moe_b_down[i])
        xp = layer_norm(DN_ALPHA * xp + moe(xp, *ffn), ln_g[i, 1], ln_b[i, 1])
        xs = layer_norm(DN_ALPHA * xs + moe(xs, *ffn), ln_g[i, 1], ln_b[i, 1])
    return (xp, xs,
            jnp.stack(p_rows, axis=2), jnp.stack(p_win, axis=0), jnp.stack(p_wkv, axis=0), jnp.stack(p_shift, axis=0),
            jnp.stack(s_rows, axis=2), jnp.stack(s_win, axis=0), jnp.stack(s_wkv, axis=0), jnp.stack(s_shift, axis=0))
```

```python
import functools

import numpy as np
import jax
import jax.numpy as jnp
from jax import lax
from jax.experimental import pallas as pl
from jax.experimental.pallas import tpu as pltpu

F32 = jnp.float32
BF16 = jnp.bfloat16

D_MODEL = 1024
DEPTH = 2
N_HEADS = 16
HEAD_DIM = 64
KV_HEADS = 4
HPG = N_HEADS // KV_HEADS
CMP_BLOCK = 32
CMP_STRIDE = 16
SEL_BLOCK = 64
N_SEL = 16
WINDOW = 512
PAGE_SIZE = 128
RWKV_HEAD = 64
RWKV_HEADS = D_MODEL // RWKV_HEAD
GN_EPS = 64e-5
N_EXPERTS = 32
TOP_K = 4
D_EXPERT = D_MODEL
SWIGLU_ALPHA = 1.702
SWIGLU_LIMIT = 7.0
LN_EPS = 1e-5
DN_ALPHA = (2 * DEPTH) ** 0.25

LANES = 128
QB = 128
NEG = -1e30
VMEM_LIMIT = 56 * 1024 * 1024
ROW_TILE = 512
MOE_ROWS = 256
WKV_CHUNK = 64


def _cparams(sem):
    return pltpu.CompilerParams(dimension_semantics=sem, vmem_limit_bytes=VMEM_LIMIT)


def _bdot(a, b):
    return jnp.dot(a.astype(BF16), b.astype(BF16), preferred_element_type=F32)


def _bdot_nt(a, b):
    return lax.dot_general(a.astype(BF16), b.astype(BF16), (((1,), (1,)), ((), ())),
                           preferred_element_type=F32)


def _bdot_tn(a, b):
    return lax.dot_general(a.astype(BF16), b.astype(BF16), (((0,), (0,)), ((), ())),
                           preferred_element_type=F32)


def _split3(x):
    hi = x.astype(BF16)
    r = x - hi.astype(F32)
    mid = r.astype(BF16)
    lo = (r - mid.astype(F32)).astype(BF16)
    return hi, mid, lo


def _dot_exact_rhs(x, w01):
    hi, mid, lo = _split3(x)
    d = lambda a: jnp.dot(a, w01, preferred_element_type=F32)
    return d(hi) + d(mid) + d(lo)


def _dot_exact_lhs(w01, x):
    hi, mid, lo = _split3(x)
    d = lambda a: jnp.dot(w01, a, preferred_element_type=F32)
    return d(hi) + d(mid) + d(lo)


def _layer_norm(z, g, b):
    mu = jnp.mean(z, -1, keepdims=True)
    zc = z - mu
    var = jnp.mean(zc * zc, -1, keepdims=True)
    return zc * lax.rsqrt(var + LN_EPS) * g + b


def _proj_kernel(x_ref, w_ref, o_ref):
    o_ref[...] = _bdot(x_ref[...], w_ref[...])


def _proj(x, w, tn, col_block0, n_out):
    m, k = x.shape
    return pl.pallas_call(
        _proj_kernel,
        grid=(n_out // tn, m // ROW_TILE),
        in_specs=[pl.BlockSpec((ROW_TILE, k), lambda j, i: (i, 0)),
                  pl.BlockSpec((k, tn), lambda j, i: (0, j + col_block0))],
        out_specs=pl.BlockSpec((ROW_TILE, tn), lambda j, i: (i, j)),
        out_shape=jax.ShapeDtypeStruct((m, n_out), F32),
        compiler_params=_cparams(("parallel", "parallel")),
        name="proj",
    )(x, w)


def _out_ln_kernel(*refs, gated):
    if gated:
        a_ref, m_ref, w_ref, res_ref, g_ref, b_ref, o_ref = refs
        a = a_ref[...] * m_ref[...]
    else:
        a_ref, w_ref, res_ref, g_ref, b_ref, o_ref = refs
        a = a_ref[...]
    z = DN_ALPHA * res_ref[...] + _bdot(a, w_ref[...])
    o_ref[...] = _layer_norm(z, g_ref[...], b_ref[...])


def _out_ln(a, mul, w, res, g, b):
    m, k = a.shape
    n = w.shape[1]
    row = lambda i: (i, 0)
    fixed = lambda i: (0, 0)
    ins = [a] + ([mul] if mul is not None else []) + [w, res, g.reshape(1, n), b.reshape(1, n)]
    specs = ([pl.BlockSpec((ROW_TILE, k), row)] * (2 if mul is not None else 1)
             + [pl.BlockSpec((k, n), fixed), pl.BlockSpec((ROW_TILE, n), row),
                pl.BlockSpec((1, n), fixed), pl.BlockSpec((1, n), fixed)])
    return pl.pallas_call(
        functools.partial(_out_ln_kernel, gated=mul is not None),
        grid=(m // ROW_TILE,),
        in_specs=specs,
        out_specs=pl.BlockSpec((ROW_TILE, n), row),
        out_shape=jax.ShapeDtypeStruct((m, n), F32),
        compiler_params=_cparams(("parallel",)),
        name="out_ln",
    )(*ins)


def _router_kernel(x_ref, w_ref, b_ref, o_ref):
    xh, xm, xl = _split3(x_ref[...])
    wh, wm, wl = _split3(w_ref[...])
    d = lambda a, c: jnp.dot(a, c, preferred_element_type=F32)
    acc = d(xh, wh) + (d(xh, wm) + d(xm, wh)) + (d(xm, wm) + d(xh, wl) + d(xl, wh))
    o_ref[...] = acc + b_ref[...]


def _router(x, w_pad, b_pad):
    m, k = x.shape
    return pl.pallas_call(
        _router_kernel,
        grid=(m // ROW_TILE,),
        in_specs=[pl.BlockSpec((ROW_TILE, k), lambda i: (i, 0)),
                  pl.BlockSpec((k, LANES), lambda i: (0, 0)),
                  pl.BlockSpec((1, LANES), lambda i: (0, 0))],
        out_specs=pl.BlockSpec((ROW_TILE, LANES), lambda i: (i, 0)),
        out_shape=jax.ShapeDtypeStruct((m, LANES), F32),
        compiler_params=_cparams(("parallel",)),
        name="router",
    )(x, w_pad, b_pad)


def _moe_kernel(be_ref, nb_ref, x_ref, wu_ref, bu_ref, wd_ref, bd_ref, gw_ref, o_ref):
    i = pl.program_id(0)

    @pl.when(i < nb_ref[0])
    def _():
        u = _bdot(x_ref[...], wu_ref[0]) + bu_ref[0]
        glu = jnp.minimum(u[:, :D_EXPERT], SWIGLU_LIMIT)
        lin = jnp.clip(u[:, D_EXPERT:], -SWIGLU_LIMIT, SWIGLU_LIMIT)
        h = glu * jax.nn.sigmoid(SWIGLU_ALPHA * glu) * (lin + 1.0)
        y = _bdot(h, wd_ref[0]) + bd_ref[0]
        o_ref[...] = y * gw_ref[...]

    @pl.when(i >= nb_ref[0])
    def _():
        o_ref[...] = jnp.zeros_like(o_ref)


def _moe_experts(xg, gw, block_e, n_used, w_up, b_up, w_down, b_down):
    r, d = xg.shape
    nblk = r // MOE_ROWS
    e, _, d2 = w_up.shape
    grid_spec = pltpu.PrefetchScalarGridSpec(
        num_scalar_prefetch=2,
        grid=(nblk,),
        in_specs=[pl.BlockSpec((MOE_ROWS, d), lambda i, be, nb: (i, 0)),
                  pl.BlockSpec((1, d, d2), lambda i, be, nb: (be[i], 0, 0)),
                  pl.BlockSpec((1, 1, d2), lambda i, be, nb: (be[i], 0, 0)),
                  pl.BlockSpec((1, d2 // 2, d), lambda i, be, nb: (be[i], 0, 0)),
                  pl.BlockSpec((1, 1, d), lambda i, be, nb: (be[i], 0, 0)),
                  pl.BlockSpec((MOE_ROWS, 1), lambda i, be, nb: (i, 0))],
        out_specs=pl.BlockSpec((MOE_ROWS, d), lambda i, be, nb: (i, 0)),
    )
    return pl.pallas_call(
        _moe_kernel,
        grid_spec=grid_spec,
        out_shape=jax.ShapeDtypeStruct((r, d), F32),
        compiler_params=_cparams(("arbitrary",)),
        name="moe_experts",
    )(block_e, n_used, xg, w_up, b_up.reshape(e, 1, d2), w_down, b_down.reshape(e, 1, d), gw)


def _combine_ln_kernel(res_ref, y_ref, g_ref, b_ref, o_ref):
    y = y_ref[0] + y_ref[1] + y_ref[2] + y_ref[3]
    o_ref[...] = _layer_norm(DN_ALPHA * res_ref[...] + y, g_ref[...], b_ref[...])


def _combine_ln(res, y4, g, b):
    m, n = res.shape
    return pl.pallas_call(
        _combine_ln_kernel,
        grid=(m // ROW_TILE,),
        in_specs=[pl.BlockSpec((ROW_TILE, n), lambda i: (i, 0)),
                  pl.BlockSpec((TOP_K, ROW_TILE, n), lambda i: (0, i, 0)),
                  pl.BlockSpec((1, n), lambda i: (0, 0)),
                  pl.BlockSpec((1, n), lambda i: (0, 0))],
        out_specs=pl.BlockSpec((ROW_TILE, n), lambda i: (i, 0)),
        out_shape=jax.ShapeDtypeStruct((m, n), F32),
        compiler_params=_cparams(("parallel",)),
        name="combine_ln",
    )(res, y4, g.reshape(1, n), b.reshape(1, n))


def _moe_layer(x, w_router, b_router, w_up, b_up, w_down, b_down, ln_g, ln_b):
    n = x.shape[0]
    wr = jnp.pad(w_router, ((0, 0), (0, LANES - N_EXPERTS)))
    br = jnp.pad(b_router, (0, LANES - N_EXPERTS)).reshape(1, LANES)
    logits = _router(x, wr, br)[:, :N_EXPERTS]
    top_v, top_e = lax.top_k(logits, TOP_K)
    gate = jax.nn.softmax(top_v, axis=-1)
    n_assign = n * TOP_K
    flat_e = top_e.reshape(-1)
    order = jnp.argsort(flat_e)
    e_sorted = flat_e[order]
    counts = jnp.zeros((N_EXPERTS,), jnp.int32).at[flat_e].add(1)
    padded = (counts + MOE_ROWS - 1) // MOE_ROWS * MOE_ROWS
    pad_end = jnp.cumsum(padded)
    pad_start = pad_end - padded
    raw_start = jnp.cumsum(counts) - counts
    dest = pad_start[e_sorted] + jnp.arange(n_assign, dtype=jnp.int32) - raw_start[e_sorted]
    n_blocks = -(-n_assign // MOE_ROWS) + N_EXPERTS
    n_rows = n_blocks * MOE_ROWS
    tok_of_row = jnp.zeros((n_rows,), jnp.int32).at[dest].set((order // TOP_K).astype(jnp.int32))
    gate_of_row = jnp.zeros((n_rows,), F32).at[dest].set(gate.reshape(-1)[order])
    row_of_assign = jnp.zeros((n_assign,), jnp.int32).at[order].set(dest.astype(jnp.int32))
    block_start = jnp.arange(n_blocks, dtype=jnp.int32) * MOE_ROWS
    block_e = jnp.minimum(jnp.sum(pad_end[None, :] <= block_start[:, None], axis=1),
                          N_EXPERTS - 1).astype(jnp.int32)
    n_used = (pad_end[-1:] // MOE_ROWS).astype(jnp.int32)
    xg = x[tok_of_row]
    yb = _moe_experts(xg, gate_of_row.reshape(n_rows, 1), block_e, n_used, w_up, b_up, w_down, b_down)
    y4 = yb[row_of_assign.reshape(n, TOP_K).T]
    return _combine_ln(x, y4, ln_g, ln_b)


def _rwkv_pre_kernel(x_ref, xp_ref, mu_ref, wrkv_ref, w0_ref, w1_ref, w2_ref, a0_ref, a1_ref, a2_ref,
                     g1_ref, g2_ref, r_ref, k_ref, v_ref, ld_ref, ag_ref, g_ref):
    x = x_ref[...]
    dx = xp_ref[...] - x
    mix = lambda s: x + dx * mu_ref[s:s + 1, :]
    r_ref[...] = _bdot(mix(0), wrkv_ref[0])
    k_ref[...] = _bdot(mix(1), wrkv_ref[1])
    v_ref[...] = _bdot(mix(2), wrkv_ref[2])
    lw = w0_ref[...] + _bdot(jnp.tanh(_bdot(mix(3), w1_ref[...])), w2_ref[...])
    z = -lw
    softplus = jnp.maximum(z, 0.0) + jnp.log1p(jnp.exp(-jnp.abs(z)))
    ld_ref[...] = -jnp.exp(-softplus - 0.5)
    ag_ref[...] = jax.nn.sigmoid(a0_ref[...] + _bdot(_bdot(mix(4), a1_ref[...]), a2_ref[...]))
    g_ref[...] = _bdot(jax.nn.sigmoid(_bdot(mix(5), g1_ref[...])), g2_ref[...])


def _rwkv_pre(x, x_prev, mu, w_rkv, w0, w1, w2, a0, a1, a2, g1, g2):
    m, d = x.shape
    row = pl.BlockSpec((ROW_TILE, d), lambda i: (i, 0))
    full = lambda a: pl.BlockSpec(a.shape, lambda i: (0,) * a.ndim)
    w0, a0 = w0.reshape(1, d), a0.reshape(1, d)
    weights = (mu, w_rkv, w0, w1, w2, a0, a1, a2, g1, g2)
    return pl.pallas_call(
        _rwkv_pre_kernel,
        grid=(m // ROW_TILE,),
        in_specs=[row, row] + [full(a) for a in weights],
        out_specs=[row] * 6,
        out_shape=[jax.ShapeDtypeStruct((m, d), F32)] * 6,
        compiler_params=_cparams(("parallel",)),
        name="rwkv_pre",
    )(x, x_prev, *weights)


def _wkv_kernel(r_ref, k_ref, v_ref, ld_ref, ag_ref, kk_ref, ka_ref, rk_ref, gg_ref, gb_ref, s0_ref,
                y_ref, sout_ref, s_ref, *, chunk):
    c = pl.program_id(1)

    @pl.when(c == 0)
    def _():
        s_ref[...] = s0_ref[0]

    row = lax.broadcasted_iota(jnp.int32, (chunk, chunk), 0)
    col = lax.broadcasted_iota(jnp.int32, (chunk, chunk), 1)
    incl = col <= row
    strict = col < row
    tri = jnp.where(incl, 1.0, 0.0).astype(BF16)
    eye = jnp.where(col == row, 1.0, 0.0).astype(F32)
    lcum_all = _dot_exact_lhs(tri, ld_ref[0])

    for h in range(RWKV_HEADS):
        sl = slice(h * RWKV_HEAD, (h + 1) * RWKV_HEAD)
        r, k, v, ld, ag = r_ref[0, :, sl], k_ref[0, :, sl], v_ref[0, :, sl], ld_ref[0, :, sl], ag_ref[0, :, sl]
        lc = lcum_all[:, sl]
        kk = k * kk_ref[:, sl]
        kk = kk / jnp.maximum(jnp.sqrt(jnp.sum(kk * kk, -1, keepdims=True)), 1e-12)
        kh = k * (1.0 + (ag - 1.0) * ka_ref[:, sl])
        b = kk * ag
        lend = lc[chunk - 1:chunk, :]
        e_neg = jnp.exp(-lc)
        a_t = -kk * jnp.exp(lc - ld)
        r_t = r * jnp.exp(lc)
        ar = jnp.concatenate([a_t, r_t], axis=0)
        bk = jnp.concatenate([b * e_neg, kh * e_neg], axis=0)
        p = _bdot_nt(ar, bk)
        l_ab = jnp.where(strict, p[:chunk, :chunk], 0.0)
        l_ak = jnp.where(strict, p[:chunk, chunk:], 0.0)
        t_rb = jnp.where(incl, p[chunk:, :chunk], 0.0)
        t_rk = jnp.where(incl, p[chunk:, chunk:], 0.0)
        s_old = s_ref[h]
        ars = _bdot_nt(ar, s_old)
        rhs = ars[:chunk] + _bdot(l_ak, v)
        inv = eye + l_ab
        lp = l_ab
        n = 2
        while n < chunk:
            lp = _bdot(lp, lp)
            inv = inv + _bdot(inv, lp)
            n *= 2
        u = _bdot(inv, rhs)
        y = ars[chunk:] + _bdot(t_rb, u) + _bdot(t_rk, v)
        e_end = jnp.exp(lend - lc)
        uv = jnp.concatenate([u, v], axis=0)
        bke = jnp.concatenate([b * e_end, kh * e_end], axis=0)
        s_new = s_old * jnp.exp(lend) + _bdot_tn(uv, bke)
        s_ref[h] = s_new
        mu_y = jnp.mean(y, -1, keepdims=True)
        yc = y - mu_y
        var_y = jnp.mean(yc * yc, -1, keepdims=True)
        yn = yc * lax.rsqrt(var_y + GN_EPS) * gg_ref[:, sl] + gb_ref[:, sl]
        bonus = jnp.sum(r * kh * rk_ref[:, sl], -1, keepdims=True) * v
        y_ref[0, :, sl] = yn + bonus

    @pl.when(c == pl.num_programs(1) - 1)
    def _():
        sout_ref[0] = s_ref[...]


def _wkv(r, k, v, ld, ag, s0, k_k, k_a, r_k, gn_g, gn_b, chunk):
    n, t, d = r.shape
    seq = pl.BlockSpec((1, chunk, d), lambda i, c: (i, c, 0))
    par = pl.BlockSpec((1, d), lambda i, c: (0, 0))
    st = pl.BlockSpec((1, RWKV_HEADS, RWKV_HEAD, RWKV_HEAD), lambda i, c: (i, 0, 0, 0))
    vec = lambda a: a.reshape(1, d)
    return pl.pallas_call(
        functools.partial(_wkv_kernel, chunk=chunk),
        grid=(n, t // chunk),
        in_specs=[seq] * 5 + [par] * 5 + [st],
        out_specs=[seq, st],
        out_shape=[jax.ShapeDtypeStruct((n, t, d), F32),
                   jax.ShapeDtypeStruct((n, RWKV_HEADS, RWKV_HEAD, RWKV_HEAD), F32)],
        scratch_shapes=[pltpu.VMEM((RWKV_HEADS, RWKV_HEAD, RWKV_HEAD), F32)],
        compiler_params=_cparams(("parallel", "arbitrary")),
        name="wkv",
    )(r, k, v, ld, ag, vec(k_k), vec(k_a), vec(r_k), vec(gn_g), vec(gn_b), s0)


PAGES_PER_SEQ = 16
SEG_PER_PAGE = PAGE_SIZE // CMP_STRIDE
N_SEG = PAGES_PER_SEQ * SEG_PER_PAGE
N_CMP = N_SEG - 1
KV_COLS = KV_HEADS * HEAD_DIM


def _gelu_tanh(x):
    return x * (0.5 * (1.0 + jnp.tanh(np.sqrt(2.0 / np.pi) * (x + 0.044715 * (x * x * x)))))


def _compress_kernel(pt_ref, *refs):
    n_slab = 2 * KV_COLS // LANES
    pages = refs[:PAGES_PER_SEQ * n_slab]
    w1c_ref, w1_ref, pe_ref, b1_ref, w2_ref, b2_ref, kc_ref, vc_ref = refs[PAGES_PER_SEQ * n_slab:]
    valid = lax.broadcasted_iota(jnp.int32, (N_SEG, HEAD_DIM), 0) < N_CMP
    for typ, out_ref in ((0, kc_ref), (1, vc_ref)):
        pe8 = jnp.broadcast_to(pe_ref[typ], (8, CMP_BLOCK * HEAD_DIM))
        c0 = _bdot(pe8, w1_ref[typ])[0:1, :] + b1_ref[typ]
        for pair in range(KV_HEADS // 2):
            cb = typ * (KV_HEADS // 2) + pair
            acc = jnp.zeros((N_SEG, 4 * HEAD_DIM), F32)
            for p in range(CMP_STRIDE):
                xp = jnp.concatenate(
                    [pages[k * n_slab + cb][0, pl.ds(p, SEG_PER_PAGE, stride=CMP_STRIDE), :]
                     for k in range(PAGES_PER_SEQ)], axis=0)
                acc = acc + _bdot(xp, w1c_ref[typ, p])
            for gg in range(2):
                g = pair * 2 + gg
                first = acc[:, gg * LANES:gg * LANES + HEAD_DIM]
                second = acc[:, gg * LANES + HEAD_DIM:(gg + 1) * LANES]
                second_next = pltpu.roll(second, N_SEG - 1, 0)
                hid = _gelu_tanh(first + second_next + c0)
                out = _bdot(hid, w2_ref[typ]) + b2_ref[typ]
                out_ref[0, :, g * HEAD_DIM:(g + 1) * HEAD_DIM] = jnp.where(valid, out, 0.0)


def _compress(pages, page_ids, n_seq, pe, w1, b1, w2, b2):
    w1c = w1.reshape(2, 2, CMP_STRIDE, HEAD_DIM, HEAD_DIM).transpose(0, 2, 3, 1, 4).reshape(
        2, CMP_STRIDE, HEAD_DIM, 2 * HEAD_DIM)
    zero = jnp.zeros_like(w1c)
    w1c = jnp.concatenate([jnp.concatenate([w1c, zero], axis=3), jnp.concatenate([zero, w1c], axis=3)],
                          axis=2)
    n_slab = 2 * KV_COLS // LANES
    page_spec = lambda k, cb: pl.BlockSpec((1, PAGE_SIZE, LANES),
                                           lambda b, pt: (pt[b * PAGES_PER_SEQ + k], 0, cb))
    full = lambda a: pl.BlockSpec(a.shape, lambda b, pt: (0,) * a.ndim)
    consts = (w1c, w1, pe.reshape(2, 1, CMP_BLOCK * HEAD_DIM), b1.reshape(2, 1, HEAD_DIM), w2,
              b2.reshape(2, 1, HEAD_DIM))
    out_spec = pl.BlockSpec((1, N_SEG, KV_COLS), lambda b, pt: (b, 0, 0))
    grid_spec = pltpu.PrefetchScalarGridSpec(
        num_scalar_prefetch=1,
        grid=(n_seq,),
        in_specs=([page_spec(k, cb) for k in range(PAGES_PER_SEQ) for cb in range(n_slab)]
                  + [full(a) for a in consts]),
        out_specs=[out_spec, out_spec],
    )
    return pl.pallas_call(
        _compress_kernel,
        grid_spec=grid_spec,
        out_shape=[jax.ShapeDtypeStruct((n_seq, N_SEG, KV_COLS), F32)] * 2,
        compiler_params=_cparams(("parallel",)),
        name="nsa_compress",
    )(page_ids, *([pages] * (PAGES_PER_SEQ * n_slab)), *consts)


def _overlap_matrix(n_blk):
    c0 = np.arange(N_CMP)[:, None] * CMP_STRIDE
    c1 = c0 + CMP_BLOCK - 1
    s0 = np.arange(n_blk)[None, :] * SEL_BLOCK
    s1 = s0 + SEL_BLOCK - 1
    ov = np.zeros((N_SEG, LANES), np.float32)
    ov[:N_CMP, :n_blk] = (c0 <= s1) & (c1 >= s0)
    return jnp.asarray(ov, BF16)


def _expand_matrix(n_keys):
    ex = (np.arange(LANES)[:, None] == (np.arange(n_keys)[None, :] // SEL_BLOCK)).astype(np.float32)
    return jnp.asarray(ex, BF16)


def _masked_softmax(s, mask):
    s = jnp.where(mask, s, NEG)
    m = jnp.max(s, -1, keepdims=True)
    e = jnp.where(mask, jnp.exp(s - m), 0.0)
    return e / jnp.maximum(jnp.sum(e, -1, keepdims=True), 1e-30)


def _select_blocks(imp, qblk, n_blk):
    lane = lax.broadcasted_iota(jnp.int32, imp.shape, 1)
    forced = (lane == 0) | (lane == qblk) | (lane == qblk - 1)
    imp = jnp.where(forced, jnp.inf, imp)
    imp = jnp.where(lane <= qblk, imp, -jnp.inf)
    rank = jnp.zeros(imp.shape, F32)
    for s in range(n_blk):
        c = imp[:, s:s + 1]
        tie_ahead = jnp.where(lane > s, 1.0, 0.0)
        rank = rank + jnp.where(c > imp, 1.0, jnp.where(c == imp, tie_ahead, 0.0))
    return jnp.where(rank < N_SEL, jnp.where(imp > -jnp.inf, 1.0, 0.0), 0.0)


def _nsa_prompt_kernel(q_ref, gt_ref, kc_ref, vc_ref, ks_ref, vs_ref, kw_ref, vw_ref, ov_ref, ex_ref,
                       o_ref, qs_ref, mx_ref, m_ref, l_ref, acc_ref, oc_ref, *, n_chunks):
    qi = pl.program_id(2)
    scale = HEAD_DIM ** -0.5
    rowq = lax.broadcasted_iota(jnp.int32, (QB, QB), 0)
    colk = lax.broadcasted_iota(jnp.int32, (QB, QB), 1)
    qpos1 = qi * QB + lax.broadcasted_iota(jnp.int32, (QB, 1), 0)
    cmp_end = colk * CMP_STRIDE + (CMP_BLOCK - 1)
    mask_c = (colk < N_CMP) & (cmp_end <= qpos1)

    m_ref[...] = jnp.full(m_ref.shape, NEG, F32)
    l_ref[...] = jnp.zeros(l_ref.shape, F32)
    acc_ref[...] = jnp.zeros(acc_ref.shape, F32)

    for g in range(2):
        for h in range(HPG):
            c0 = (g * HPG + h) * HEAD_DIM
            qs_ref[g, h * QB:(h + 1) * QB, :] = (q_ref[0, :, c0:c0 + HEAD_DIM] * scale).astype(BF16)
        gs = slice(g * HEAD_DIM, (g + 1) * HEAD_DIM)
        s = _bdot_nt(qs_ref[g], kc_ref[0, :, gs]).reshape(HPG, QB, QB)
        p = _masked_softmax(s, mask_c[None])
        oc_ref[g] = _bdot(p.reshape(HPG * QB, QB), vc_ref[0, :, gs])
        imp = _dot_exact_rhs(p[0] + p[1] + p[2] + p[3], ov_ref[...])
        sel = _select_blocks(imp, qpos1 // SEL_BLOCK, n_chunks * QB // SEL_BLOCK)
        key_mask = jnp.dot(sel.astype(BF16), ex_ref[...], preferred_element_type=F32)
        for c in range(n_chunks):
            mx_ref[g, c] = key_mask[:, c * QB:(c + 1) * QB]

    def online_update(br, g, k, v, mask):
        s = _bdot_nt(qs_ref[g], k).reshape(HPG, QB, QB)
        s = jnp.where(mask[None], s, NEG)
        m_old = m_ref[br, g]
        m_new = jnp.maximum(m_old, jnp.max(s, -1, keepdims=True))
        alpha = jnp.exp(m_old - m_new)
        p = jnp.where(mask[None], jnp.exp(s - m_new), 0.0)
        l_ref[br, g] = alpha * l_ref[br, g] + jnp.sum(p, -1, keepdims=True)
        acc_ref[br, g] = (alpha.reshape(HPG * QB, 1) * acc_ref[br, g]
                          + _bdot(p.reshape(HPG * QB, QB), v))
        m_ref[br, g] = m_new

    def sel_body(kc, carry):
        off = pl.multiple_of(kc * QB, QB)
        causal = (kc - qi) * QB + colk - rowq <= 0
        for g in range(2):
            k = ks_ref[0, pl.ds(off, QB), pl.ds(g * HEAD_DIM, HEAD_DIM)]
            v = vs_ref[0, pl.ds(off, QB), pl.ds(g * HEAD_DIM, HEAD_DIM)]
            mask = causal & (mx_ref[g, kc] > 0.5)
            online_update(0, g, k, v, mask)
        return carry

    lax.fori_loop(0, qi + 1, sel_body, 0)

    def win_body(kc, carry):
        off = pl.multiple_of(kc * QB, QB)
        dist = (qi - kc) * QB + rowq - colk
        mask = (dist >= 0) & (dist < WINDOW)
        for g in range(2):
            k = kw_ref[0, pl.ds(off, QB), pl.ds(g * HEAD_DIM, HEAD_DIM)]
            v = vw_ref[0, pl.ds(off, QB), pl.ds(g * HEAD_DIM, HEAD_DIM)]
            online_update(1, g, k, v, mask)
        return carry

    lax.fori_loop(jnp.maximum(qi - WINDOW // QB, 0), qi + 1, win_body, 0)

    gt = jax.nn.sigmoid(gt_ref[0])
    for g in range(2):
        o_sel = acc_ref[0, g] / jnp.maximum(l_ref[0, g], 1e-30).reshape(HPG * QB, 1)
        o_win = acc_ref[1, g] / jnp.maximum(l_ref[1, g], 1e-30).reshape(HPG * QB, 1)
        o_cmp = oc_ref[g]
        for h in range(HPG):
            rows = slice(h * QB, (h + 1) * QB)
            gc = (g * HPG + h) * 3
            o = (gt[:, gc:gc + 1] * o_cmp[rows] + gt[:, gc + 1:gc + 2] * o_sel[rows]
                 + gt[:, gc + 2:gc + 3] * o_win[rows])
            c0 = (g * HPG + h) * HEAD_DIM
            o_ref[0, :, c0:c0 + HEAD_DIM] = o


def _nsa_prompt(q, gates, kc, vc, rows, win):
    b, t, _ = q.shape
    nq = t // QB
    pair = 2 * HEAD_DIM
    blk = lambda shape, fn: pl.BlockSpec(shape, fn)
    seq_cols = lambda cb: blk((1, t, pair), lambda i, gp, qi: (i, 0, cb + gp))
    return pl.pallas_call(
        functools.partial(_nsa_prompt_kernel, n_chunks=nq),
        grid=(b, KV_HEADS // 2, nq),
        in_specs=[blk((1, QB, 2 * HPG * HEAD_DIM), lambda i, gp, qi: (i, qi, gp)),
                  blk((1, QB, LANES), lambda i, gp, qi: (i, qi, gp)),
                  blk((1, N_SEG, pair), lambda i, gp, qi: (i, 0, gp)),
                  blk((1, N_SEG, pair), lambda i, gp, qi: (i, 0, gp)),
                  seq_cols(4), seq_cols(6),
                  seq_cols(0), seq_cols(2),
                  blk((N_SEG, LANES), lambda i, gp, qi: (0, 0)),
                  blk((LANES, t), lambda i, gp, qi: (0, 0))],
        out_specs=blk((1, QB, 2 * HPG * HEAD_DIM), lambda i, gp, qi: (i, qi, gp)),
        out_shape=jax.ShapeDtypeStruct((b, t, N_HEADS * HEAD_DIM), F32),
        scratch_shapes=[pltpu.VMEM((2, HPG * QB, HEAD_DIM), BF16),
                        pltpu.VMEM((2, nq, QB, QB), F32),
                        pltpu.VMEM((2, 2, HPG, QB, 1), F32),
                        pltpu.VMEM((2, 2, HPG, QB, 1), F32),
                        pltpu.VMEM((2, 2, HPG * QB, HEAD_DIM), F32),
                        pltpu.VMEM((2, HPG * QB, HEAD_DIM), F32)],
        compiler_params=_cparams(("parallel", "parallel", "arbitrary")),
        name="nsa_prompt",
    )(q, gates, kc, vc, rows, rows, win, win, _overlap_matrix(t // SEL_BLOCK), _expand_matrix(t))


TQ = 8


def _nsa_sample_kernel(pt_ref, *refs, past_len, n_new):
    pages = refs[:PAGES_PER_SEQ]
    q_ref, gt_ref, kc_ref, vc_ref, new_ref, wb_ref, wn_ref, ov_ref, ex_ref, o_ref = refs[PAGES_PER_SEQ:]
    scale = HEAD_DIM ** -0.5
    rows = HPG * TQ
    tok1 = lax.broadcasted_iota(jnp.int32, (TQ, 1), 0)
    qpos1 = past_len + tok1
    lane = lax.broadcasted_iota(jnp.int32, (TQ, LANES), 1)
    mask_c = (lane < N_CMP) & (lane * CMP_STRIDE + (CMP_BLOCK - 1) <= qpos1)
    newj = lax.broadcasted_iota(jnp.int32, (TQ, TQ), 1)
    newt = lax.broadcasted_iota(jnp.int32, (TQ, TQ), 0)
    mask_new = (newj < n_new) & (newj <= newt)
    wcache = wb_ref.shape[1]
    wi = lax.broadcasted_iota(jnp.int32, (TQ, wcache), 1)
    wpos = past_len - wcache + wi
    mask_w = (wpos <= qpos1) & (wpos > qpos1 - WINDOW)
    n_blk = -(-(past_len + n_new) // SEL_BLOCK)
    gt = jax.nn.sigmoid(gt_ref[0])

    def bcast(mask):
        return jnp.broadcast_to(mask[None], (HPG,) + mask.shape)

    def two_part_attention(s_a, mask_a, s_b, mask_b):
        s_a = jnp.where(mask_a, s_a, NEG)
        s_b = jnp.where(mask_b, s_b, NEG)
        m = jnp.maximum(jnp.max(s_a, -1, keepdims=True), jnp.max(s_b, -1, keepdims=True))
        e_a = jnp.where(mask_a, jnp.exp(s_a - m), 0.0)
        e_b = jnp.where(mask_b, jnp.exp(s_b - m), 0.0)
        den = jnp.maximum(jnp.sum(e_a, -1, keepdims=True) + jnp.sum(e_b, -1, keepdims=True), 1e-30)
        return e_a / den, e_b / den

    for g in range(KV_HEADS):
        gs = slice(g * HEAD_DIM, (g + 1) * HEAD_DIM)
        vs_cols = slice(KV_COLS + g * HEAD_DIM, KV_COLS + (g + 1) * HEAD_DIM)
        qg = jnp.concatenate(
            [q_ref[0, :, (g * HPG + h) * HEAD_DIM:(g * HPG + h + 1) * HEAD_DIM] for h in range(HPG)],
            axis=0) * scale
        qg = qg.astype(BF16)
        s = _bdot_nt(qg, kc_ref[0, :, gs]).reshape(HPG, TQ, N_SEG)
        p = _masked_softmax(s, mask_c[None])
        o_cmp = _bdot(p.reshape(rows, N_SEG), vc_ref[0, :, gs])
        imp = _dot_exact_rhs(p[0] + p[1] + p[2] + p[3], ov_ref[...])
        sel = _select_blocks(imp, qpos1 // SEL_BLOCK, n_blk)
        past_mask = jnp.dot(sel.astype(BF16), ex_ref[...], preferred_element_type=F32) > 0.5
        new_blk = past_len // SEL_BLOCK
        mask_sn = mask_new & (sel[:, new_blk:new_blk + 1] > 0.5)
        s_past = jnp.concatenate([_bdot_nt(qg, pg[0, :, gs]) for pg in pages], axis=1)
        s_new = _bdot_nt(qg, new_ref[0, :, 2 * KV_COLS + g * HEAD_DIM:2 * KV_COLS + (g + 1) * HEAD_DIM])
        p_past, p_new = two_part_attention(s_past.reshape(HPG, TQ, -1), bcast(past_mask),
                                           s_new.reshape(HPG, TQ, TQ), bcast(mask_sn))
        p_past = p_past.reshape(rows, -1)
        o_sel = _bdot(p_new.reshape(rows, TQ),
                      new_ref[0, :, 3 * KV_COLS + g * HEAD_DIM:3 * KV_COLS + (g + 1) * HEAD_DIM])
        for k, pg in enumerate(pages):
            o_sel = o_sel + _bdot(p_past[:, k * PAGE_SIZE:(k + 1) * PAGE_SIZE], pg[0, :, vs_cols])
        s_wb = _bdot_nt(qg, wb_ref[0, :, gs])
        s_wn = _bdot_nt(qg, wn_ref[0, :, gs])
        p_wb, p_wn = two_part_attention(s_wb.reshape(HPG, TQ, wcache), bcast(mask_w),
                                        s_wn.reshape(HPG, TQ, TQ), bcast(mask_new))
        o_win = (_bdot(p_wb.reshape(rows, wcache), wb_ref[0, :, vs_cols])
                 + _bdot(p_wn.reshape(rows, TQ), wn_ref[0, :, vs_cols]))
        for h in range(HPG):
            r8 = slice(h * TQ, (h + 1) * TQ)
            gc = (g // 2) * LANES + ((g % 2) * HPG + h) * 3
            o = (gt[:, gc:gc + 1] * o_cmp[r8] + gt[:, gc + 1:gc + 2] * o_sel[r8]
                 + gt[:, gc + 2:gc + 3] * o_win[r8])
            c0 = (g * HPG + h) * HEAD_DIM
            o_ref[0, :, c0:c0 + HEAD_DIM] = o


def _nsa_sample(pages, page_ids, q, gates, kc, vc, new_rows, win_buf, win_new, past_len, n_new):
    n = q.shape[0]
    wc = win_buf.shape[1]
    n_blk = -(-(past_len + n_new) // SEL_BLOCK)
    page_spec = lambda k: pl.BlockSpec((1, PAGE_SIZE, 2 * KV_COLS),
                                       lambda b, pt: (pt[b * PAGES_PER_SEQ + k], 0, 1))
    seq = lambda a: pl.BlockSpec((1,) + a.shape[1:], lambda b, pt: (b,) + (0,) * (a.ndim - 1))
    full = lambda a: pl.BlockSpec(a.shape, lambda b, pt: (0,) * a.ndim)
    ov, ex = _overlap_matrix(n_blk), _expand_matrix(past_len)
    per_seq = (q, gates, kc, vc, new_rows, win_buf, win_new)
    grid_spec = pltpu.PrefetchScalarGridSpec(
        num_scalar_prefetch=1,
        grid=(n,),
        in_specs=([page_spec(k) for k in range(PAGES_PER_SEQ)] + [seq(a) for a in per_seq]
                  + [full(ov), full(ex)]),
        out_specs=pl.BlockSpec((1, TQ, N_HEADS * HEAD_DIM), lambda b, pt: (b, 0, 0)),
    )
    return pl.pallas_call(
        functools.partial(_nsa_sample_kernel, past_len=past_len, n_new=n_new),
        grid_spec=grid_spec,
        out_shape=jax.ShapeDtypeStruct((n, TQ, N_HEADS * HEAD_DIM), F32),
        compiler_params=_cparams(("parallel",)),
        name="nsa_sample",
    )(page_ids, *([pages] * PAGES_PER_SEQ), *per_seq, ov, ex)


WKV_SAMPLE_CHUNK = 16


def _pad_rows(a, n):
    return jnp.pad(a, ((0, 0), (0, n - a.shape[1]), (0, 0)))


def _nsa_layer(x, n_prompt, b, t, db, ds, pages, page_table, win_cache, w_in, w_out, cmp, ln_g, ln_b):
    d = x.shape[1]
    qd = N_HEADS * HEAD_DIM
    q = _proj(x, w_in, 512, 0, qd)
    rows = _proj(x, w_in, 512, 2, 4 * KV_COLS)
    win = _proj(x, w_in, 512, 4, 2 * KV_COLS)
    n_gate = 2 * HPG * 3
    g0 = qd + 6 * KV_COLS
    wg = jnp.concatenate(
        [jnp.pad(w_in[:, g0 + p * n_gate:g0 + (p + 1) * n_gate], ((0, 0), (0, LANES - n_gate)))
         for p in range(KV_HEADS // 2)], axis=1)
    gates = _proj(x, wg, wg.shape[1], 0, wg.shape[1])
    past_len = page_table.shape[1] * PAGE_SIZE

    q_p, rows_p, win_p, gates_p = (a[:n_prompt].reshape(b, t, -1) for a in (q, rows, win, gates))
    prompt_pages = rows_p.reshape(b * PAGES_PER_SEQ, PAGE_SIZE, 4 * KV_COLS)
    kc_p, vc_p = _compress(prompt_pages, jnp.arange(b * PAGES_PER_SEQ, dtype=jnp.int32), b, *cmp)
    o_p = _nsa_prompt(q_p, gates_p, kc_p, vc_p, rows_p, win_p)

    q_s, rows_s, win_s, gates_s = (a[n_prompt:].reshape(db, ds, -1) for a in (q, rows, win, gates))
    page_ids = page_table.reshape(-1).astype(jnp.int32)
    kc_s, vc_s = _compress(pages, page_ids, db, *cmp)
    win_buf = win_cache.reshape(db, win_cache.shape[1], 2 * KV_COLS)
    o_s = _nsa_sample(pages, page_ids, _pad_rows(q_s, TQ), _pad_rows(gates_s, TQ), kc_s, vc_s,
                      _pad_rows(rows_s, TQ), win_buf, _pad_rows(win_s, TQ), past_len, ds)[:, :ds]

    o = jnp.concatenate([o_p.reshape(n_prompt, qd), o_s.reshape(db * ds, qd)], axis=0)
    x_new = _out_ln(o, None, w_out, x, ln_g, ln_b)
    keep = min(WINDOW, t)
    outs = (rows_p.reshape(b, t, 4, KV_HEADS, HEAD_DIM),
            win_p[:, t - keep:].reshape(b, keep, 2, KV_HEADS, HEAD_DIM),
            rows_s.reshape(db, ds, 4, KV_HEADS, HEAD_DIM),
            jnp.concatenate([win_buf, win_s], axis=1)[:, ds:].reshape(db, -1, 2, KV_HEADS, HEAD_DIM))
    return x_new, outs


def _rwkv_layer(x, n_prompt, b, t, db, ds, shift_s, state_s, mu, w_rkv, w0, w1, w2, a0, a1, a2, g1, g2,
                k_k, k_a, r_k, gn_g, gn_b, w_out, ln_g, ln_b):
    d = x.shape[1]
    xp = x[:n_prompt].reshape(b, t, d)
    xs = x[n_prompt:].reshape(db, ds, d)
    prev_p = jnp.concatenate([jnp.zeros((b, 1, d), x.dtype), xp[:, :-1]], axis=1)
    prev_s = jnp.concatenate([shift_s[:, None].astype(x.dtype), xs[:, :-1]], axis=1)
    x_prev = jnp.concatenate([prev_p.reshape(n_prompt, d), prev_s.reshape(db * ds, d)], axis=0)
    streams = _rwkv_pre(x, x_prev, mu, w_rkv, w0, w1, w2, a0, a1, a2, g1, g2)
    r, k, v, ld, ag, g = streams
    head_params = (k_k, k_a, r_k.reshape(d), gn_g, gn_b)
    seq_p = [a[:n_prompt].reshape(b, t, d) for a in (r, k, v, ld, ag)]
    zero_state = jnp.zeros((b, RWKV_HEADS, RWKV_HEAD, RWKV_HEAD), F32)
    y_p, wkv_p = _wkv(*seq_p, zero_state, *head_params, WKV_CHUNK)
    seq_s = [_pad_rows(a[n_prompt:].reshape(db, ds, d), WKV_SAMPLE_CHUNK) for a in (r, k, v, ld, ag)]
    y_s, wkv_s = _wkv(*seq_s, state_s.astype(F32), *head_params, WKV_SAMPLE_CHUNK)
    y = jnp.concatenate([y_p.reshape(n_prompt, d), y_s[:, :ds].reshape(db * ds, d)], axis=0)
    x_new = _out_ln(y, g, w_out, x, ln_g, ln_b)
    return x_new, (wkv_p, xp[:, -1], wkv_s.astype(state_s.dtype), xs[:, -1])


def kernel(x_prompt, x_sample, cache_nsa_paged, cache_nsa_win, state_rwkv_wkv, state_rwkv_shift, page_table, nsa_w_in, nsa_w_out, nsa_cmp_pe, nsa_cmp_w1, nsa_cmp_b1, nsa_cmp_w2, nsa_cmp_b2, rwkv_mu, rwkv_w_rkv, rwkv_w0, rwkv_w1, rwkv_w2, rwkv_a0, rwkv_a1, rwkv_a2, rwkv_g1, rwkv_g2, rwkv_k_k, rwkv_k_a, rwkv_r_k, rwkv_gn_g, rwkv_gn_b, rwkv_w_out, ln_g, ln_b, moe_w_router, moe_b_router, moe_w_up, moe_b_up, moe_w_down, moe_b_down):
    b, t, d = x_prompt.shape
    db, ds, _ = x_sample.shape
    n_prompt = b * t
    x = jnp.concatenate([x_prompt.reshape(n_prompt, d), x_sample.reshape(db * ds, d)], axis=0)
    n_phys = cache_nsa_paged.shape[0]
    nsa_outs, rwkv_outs = [], []
    for i in range(DEPTH):
        j = i // 2
        if i % 2 == 0:
            pages = cache_nsa_paged[:, :, j].reshape(n_phys, PAGE_SIZE, 4 * KV_COLS)
            cmp = (nsa_cmp_pe[j], nsa_cmp_w1[j], nsa_cmp_b1[j], nsa_cmp_w2[j], nsa_cmp_b2[j])
            x, outs = _nsa_layer(x, n_prompt, b, t, db, ds, pages, page_table, cache_nsa_win[j],
                                 nsa_w_in[j], nsa_w_out[j], cmp, ln_g[i, 0], ln_b[i, 0])
            nsa_outs.append(outs)
        else:
            x, outs = _rwkv_layer(x, n_prompt, b, t, db, ds, state_rwkv_shift[j], state_rwkv_wkv[j],
                                  rwkv_mu[j], rwkv_w_rkv[j], rwkv_w0[j], rwkv_w1[j], rwkv_w2[j],
                                  rwkv_a0[j], rwkv_a1[j], rwkv_a2[j], rwkv_g1[j], rwkv_g2[j],
                                  rwkv_k_k[j], rwkv_k_a[j], rwkv_r_k[j], rwkv_gn_g[j], rwkv_gn_b[j],
                                  rwkv_w_out[j], ln_g[i, 0], ln_b[i, 0])
            rwkv_outs.append(outs)
        x = _moe_layer(x, moe_w_router[i], moe_b_router[i], moe_w_up[i], moe_b_up[i], moe_w_down[i],
                       moe_b_down[i], ln_g[i, 1], ln_b[i, 1])
    p_rows, p_win, s_rows, s_win = (jnp.stack(z, axis=k) for z, k in zip(zip(*nsa_outs), (2, 0, 2, 0)))
    p_wkv, p_shift, s_wkv, s_shift = (jnp.stack(z, axis=0) for z in zip(*rwkv_outs))
    return (x[:n_prompt].reshape(b, t, d), x[n_prompt:].reshape(db, ds, d),
            p_rows, p_win, p_wkv, p_shift, s_rows, s_win, s_wkv, s_shift)
```

```python
import functools

import numpy as np
import jax
import jax.numpy as jnp
from jax import lax
from jax.experimental import pallas as pl
from jax.experimental.pallas import tpu as pltpu

F32 = jnp.float32
BF16 = jnp.bfloat16

D_MODEL = 1024
DEPTH = 2
N_HEADS = 16
HEAD_DIM = 64
KV_HEADS = 4
HPG = N_HEADS // KV_HEADS
CMP_BLOCK = 32
CMP_STRIDE = 16
SEL_BLOCK = 64
N_SEL = 16
WINDOW = 512
PAGE_SIZE = 128
RWKV_HEAD = 64
RWKV_HEADS = D_MODEL // RWKV_HEAD
GN_EPS = 64e-5
N_EXPERTS = 32
TOP_K = 4
D_EXPERT = D_MODEL
SWIGLU_ALPHA = 1.702
SWIGLU_LIMIT = 7.0
LN_EPS = 1e-5
DN_ALPHA = (2 * DEPTH) ** 0.25

LANES = 128
QB = 128
NEG = -1e30
VMEM_LIMIT = 56 * 1024 * 1024
ROW_TILE = 512
MOE_ROWS = 256
WKV_CHUNK = 64


def _cparams(sem):
    return pltpu.CompilerParams(dimension_semantics=sem, vmem_limit_bytes=VMEM_LIMIT)


def _bdot(a, b):
    return jnp.dot(a.astype(BF16), b.astype(BF16), preferred_element_type=F32)


def _bdot_nt(a, b):
    return lax.dot_general(a.astype(BF16), b.astype(BF16), (((1,), (1,)), ((), ())),
                           preferred_element_type=F32)


def _bdot_tn(a, b):
    return lax.dot_general(a.astype(BF16), b.astype(BF16), (((0,), (0,)), ((), ())),
                           preferred_element_type=F32)


def _bmm(a, b):
    return lax.dot_general(a.astype(BF16), b.astype(BF16), (((2,), (1,)), ((0,), (0,))),
                           preferred_element_type=F32)


def _bmm_nt(a, b):
    return lax.dot_general(a.astype(BF16), b.astype(BF16), (((2,), (2,)), ((0,), (0,))),
                           preferred_element_type=F32)


def _split3(x):
    hi = x.astype(BF16)
    r = x - hi.astype(F32)
    mid = r.astype(BF16)
    lo = (r - mid.astype(F32)).astype(BF16)
    return hi, mid, lo


def _dot_exact_rhs(x, w01):
    hi, mid, lo = _split3(x)
    d = lambda a: jnp.dot(a, w01, preferred_element_type=F32)
    return d(hi) + d(mid) + d(lo)


def _dot_exact_lhs(w01, x):
    hi, mid, lo = _split3(x)
    d = lambda a: jnp.dot(w01, a, preferred_element_type=F32)
    return d(hi) + d(mid) + d(lo)


def _layer_norm(z, g, b):
    mu = jnp.mean(z, -1, keepdims=True)
    zc = z - mu
    var = jnp.mean(zc * zc, -1, keepdims=True)
    return zc * lax.rsqrt(var + LN_EPS) * g + b


def _proj_kernel(x_ref, w_ref, o_ref):
    o_ref[...] = _bdot(x_ref[...], w_ref[...])


def _proj(x, w, tn, col_block0, n_out):
    m, k = x.shape
    return pl.pallas_call(
        _proj_kernel,
        grid=(n_out // tn, m // ROW_TILE),
        in_specs=[pl.BlockSpec((ROW_TILE, k), lambda j, i: (i, 0)),
                  pl.BlockSpec((k, tn), lambda j, i: (0, j + col_block0))],
        out_specs=pl.BlockSpec((ROW_TILE, tn), lambda j, i: (i, j)),
        out_shape=jax.ShapeDtypeStruct((m, n_out), F32),
        compiler_params=_cparams(("parallel", "parallel")),
        name="proj",
    )(x, w)


def _out_ln_kernel(*refs, gated):
    if gated:
        a_ref, m_ref, w_ref, res_ref, g_ref, b_ref, o_ref = refs
        a = a_ref[...] * m_ref[...]
    else:
        a_ref, w_ref, res_ref, g_ref, b_ref, o_ref = refs
        a = a_ref[...]
    z = DN_ALPHA * res_ref[...] + _bdot(a, w_ref[...])
    o_ref[...] = _layer_norm(z, g_ref[...], b_ref[...])


def _out_ln(a, mul, w, res, g, b):
    m, k = a.shape
    n = w.shape[1]
    row = lambda i: (i, 0)
    fixed = lambda i: (0, 0)
    ins = [a] + ([mul] if mul is not None else []) + [w, res, g.reshape(1, n), b.reshape(1, n)]
    specs = ([pl.BlockSpec((ROW_TILE, k), row)] * (2 if mul is not None else 1)
             + [pl.BlockSpec((k, n), fixed), pl.BlockSpec((ROW_TILE, n), row),
                pl.BlockSpec((1, n), fixed), pl.BlockSpec((1, n), fixed)])
    return pl.pallas_call(
        functools.partial(_out_ln_kernel, gated=mul is not None),
        grid=(m // ROW_TILE,),
        in_specs=specs,
        out_specs=pl.BlockSpec((ROW_TILE, n), row),
        out_shape=jax.ShapeDtypeStruct((m, n), F32),
        compiler_params=_cparams(("parallel",)),
        name="out_ln",
    )(*ins)


def _router_kernel(x_ref, w_ref, b_ref, o_ref):
    xh, xm, xl = _split3(x_ref[...])
    wh, wm, wl = _split3(w_ref[...])
    d = lambda a, c: jnp.dot(a, c, preferred_element_type=F32)
    acc = d(xh, wh) + (d(xh, wm) + d(xm, wh)) + (d(xm, wm) + d(xh, wl) + d(xl, wh))
    o_ref[...] = acc + b_ref[...]


def _router(x, w_pad, b_pad):
    m, k = x.shape
    return pl.pallas_call(
        _router_kernel,
        grid=(m // ROW_TILE,),
        in_specs=[pl.BlockSpec((ROW_TILE, k), lambda i: (i, 0)),
                  pl.BlockSpec((k, LANES), lambda i: (0, 0)),
                  pl.BlockSpec((1, LANES), lambda i: (0, 0))],
        out_specs=pl.BlockSpec((ROW_TILE, LANES), lambda i: (i, 0)),
        out_shape=jax.ShapeDtypeStruct((m, LANES), F32),
        compiler_params=_cparams(("parallel",)),
        name="router",
    )(x, w_pad, b_pad)


def _moe_kernel(be_ref, nb_ref, x_ref, wu_ref, bu_ref, wd_ref, bd_ref, gw_ref, o_ref):
    i = pl.program_id(0)

    @pl.when(i < nb_ref[0])
    def _():
        u = _bdot(x_ref[...], wu_ref[0]) + bu_ref[0]
        glu = jnp.minimum(u[:, :D_EXPERT], SWIGLU_LIMIT)
        lin = jnp.clip(u[:, D_EXPERT:], -SWIGLU_LIMIT, SWIGLU_LIMIT)
        h = glu * jax.nn.sigmoid(SWIGLU_ALPHA * glu) * (lin + 1.0)
        y = _bdot(h, wd_ref[0]) + bd_ref[0]
        o_ref[...] = y * gw_ref[...]

    @pl.when(i >= nb_ref[0])
    def _():
        o_ref[...] = jnp.zeros_like(o_ref)


def _moe_experts(xg, gw, block_e, n_used, w_up, b_up, w_down, b_down):
    r, d = xg.shape
    nblk = r // MOE_ROWS
    e, _, d2 = w_up.shape
    grid_spec = pltpu.PrefetchScalarGridSpec(
        num_scalar_prefetch=2,
        grid=(nblk,),
        in_specs=[pl.BlockSpec((MOE_ROWS, d), lambda i, be, nb: (i, 0)),
                  pl.BlockSpec((1, d, d2), lambda i, be, nb: (be[i], 0, 0)),
                  pl.BlockSpec((1, 1, d2), lambda i, be, nb: (be[i], 0, 0)),
                  pl.BlockSpec((1, d2 // 2, d), lambda i, be, nb: (be[i], 0, 0)),
                  pl.BlockSpec((1, 1, d), lambda i, be, nb: (be[i], 0, 0)),
                  pl.BlockSpec((MOE_ROWS, 1), lambda i, be, nb: (i, 0))],
        out_specs=pl.BlockSpec((MOE_ROWS, d), lambda i, be, nb: (i, 0)),
    )
    return pl.pallas_call(
        _moe_kernel,
        grid_spec=grid_spec,
        out_shape=jax.ShapeDtypeStruct((r, d), F32),
        compiler_params=_cparams(("arbitrary",)),
        name="moe_experts",
    )(block_e, n_used, xg, w_up, b_up.reshape(e, 1, d2), w_down, b_down.reshape(e, 1, d), gw)


def _combine_ln_kernel(res_ref, y_ref, g_ref, b_ref, o_ref):
    y = y_ref[0] + y_ref[1] + y_ref[2] + y_ref[3]
    o_ref[...] = _layer_norm(DN_ALPHA * res_ref[...] + y, g_ref[...], b_ref[...])


def _combine_ln(res, y4, g, b):
    m, n = res.shape
    return pl.pallas_call(
        _combine_ln_kernel,
        grid=(m // ROW_TILE,),
        in_specs=[pl.BlockSpec((ROW_TILE, n), lambda i: (i, 0)),
                  pl.BlockSpec((TOP_K, ROW_TILE, n), lambda i: (0, i, 0)),
                  pl.BlockSpec((1, n), lambda i: (0, 0)),
                  pl.BlockSpec((1, n), lambda i: (0, 0))],
        out_specs=pl.BlockSpec((ROW_TILE, n), lambda i: (i, 0)),
        out_shape=jax.ShapeDtypeStruct((m, n), F32),
        compiler_params=_cparams(("parallel",)),
        name="combine_ln",
    )(res, y4, g.reshape(1, n), b.reshape(1, n))


def _moe_layer(x, w_router, b_router, w_up, b_up, w_down, b_down, ln_g, ln_b):
    n = x.shape[0]
    wr = jnp.pad(w_router, ((0, 0), (0, LANES - N_EXPERTS)))
    br = jnp.pad(b_router, (0, LANES - N_EXPERTS)).reshape(1, LANES)
    logits = _router(x, wr, br)[:, :N_EXPERTS]
    top_v, top_e = lax.top_k(logits, TOP_K)
    gate = jax.nn.softmax(top_v, axis=-1)
    n_assign = n * TOP_K
    flat_e = top_e.reshape(-1).astype(jnp.int32)
    iota = jnp.arange(n_assign, dtype=jnp.int32)
    e_sorted, order = lax.sort((flat_e, iota), num_keys=1)
    _, sorted_pos = lax.sort((order, iota), num_keys=1)
    experts = jnp.arange(N_EXPERTS, dtype=jnp.int32)
    counts = jnp.sum((flat_e[None, :] == experts[:, None]).astype(jnp.int32), axis=1)
    raw_start = jnp.cumsum(counts) - counts
    padded = (counts + MOE_ROWS - 1) // MOE_ROWS * MOE_ROWS
    pad_end = jnp.cumsum(padded)
    pad_start = pad_end - padded
    n_blocks = -(-n_assign // MOE_ROWS) + N_EXPERTS
    n_rows = n_blocks * MOE_ROWS
    block_start = jnp.arange(n_blocks, dtype=jnp.int32) * MOE_ROWS
    block_e = jnp.minimum(jnp.sum(pad_end[None, :] <= block_start[:, None], axis=1),
                          N_EXPERTS - 1).astype(jnp.int32)
    n_used = (pad_end[-1:] // MOE_ROWS).astype(jnp.int32)
    j = (block_start - pad_start[block_e])[:, None] + jnp.arange(MOE_ROWS, dtype=jnp.int32)[None, :]
    valid = (j < counts[block_e][:, None]).reshape(-1)
    src = jnp.clip(raw_start[block_e][:, None] + j, 0, n_assign - 1).reshape(-1)
    assign_of_row = order[src]
    tok_of_row = jnp.where(valid, assign_of_row // TOP_K, 0)
    gate_of_row = jnp.where(valid, gate.reshape(-1)[assign_of_row], 0.0)
    row_of_assign = (pad_start - raw_start)[flat_e] + sorted_pos
    xg = x[tok_of_row]
    yb = _moe_experts(xg, gate_of_row.reshape(n_rows, 1), block_e, n_used, w_up, b_up, w_down, b_down)
    y4 = yb[row_of_assign.reshape(n, TOP_K).T]
    return _combine_ln(x, y4, ln_g, ln_b)


def _rwkv_pre_kernel(x_ref, carry_ref, xps_ref, mu_ref, wrkv_ref, w0_ref, w1_ref, w2_ref, a0_ref, a1_ref,
                     a2_ref, g1_ref, g2_ref, r_ref, k_ref, v_ref, ld_ref, ag_ref, g_ref, *, prompt_tiles):
    x = x_ref[...]
    first = lax.broadcasted_iota(jnp.int32, x.shape, 0) == 0
    x_prev = jnp.where(first, carry_ref[0], pltpu.roll(x, 1, 0))
    x_prev = jnp.where(pl.program_id(0) >= prompt_tiles, xps_ref[...], x_prev)
    dx = x_prev - x
    mix = lambda s: x + dx * mu_ref[s:s + 1, :]
    r_ref[...] = _bdot(mix(0), wrkv_ref[0])
    k_ref[...] = _bdot(mix(1), wrkv_ref[1])
    v_ref[...] = _bdot(mix(2), wrkv_ref[2])
    lw = w0_ref[...] + _bdot(jnp.tanh(_bdot(mix(3), w1_ref[...])), w2_ref[...])
    z = -lw
    softplus = jnp.maximum(z, 0.0) + jnp.log1p(jnp.exp(-jnp.abs(z)))
    ld_ref[...] = -jnp.exp(-softplus - 0.5)
    ag_ref[...] = jax.nn.sigmoid(a0_ref[...] + _bdot(_bdot(mix(4), a1_ref[...]), a2_ref[...]))
    g_ref[...] = _bdot(jax.nn.sigmoid(_bdot(mix(5), g1_ref[...])), g2_ref[...])


def _rwkv_pre(x, carry, x_prev_sample, mu, w_rkv, w0, w1, w2, a0, a1, a2, g1, g2):
    m, d = x.shape
    prompt_tiles = (m - x_prev_sample.shape[0]) // ROW_TILE
    row = pl.BlockSpec((ROW_TILE, d), lambda i: (i, 0))
    full = lambda a: pl.BlockSpec(a.shape, lambda i: (0,) * a.ndim)
    w0, a0 = w0.reshape(1, d), a0.reshape(1, d)
    weights = (mu, w_rkv, w0, w1, w2, a0, a1, a2, g1, g2)
    return pl.pallas_call(
        functools.partial(_rwkv_pre_kernel, prompt_tiles=prompt_tiles),
        grid=(m // ROW_TILE,),
        in_specs=[row, pl.BlockSpec((1, 1, d), lambda i: (i, 0, 0)),
                  pl.BlockSpec((ROW_TILE, d), lambda i: (jnp.maximum(i - prompt_tiles, 0), 0))]
                 + [full(a) for a in weights],
        out_specs=[row] * 6,
        out_shape=[jax.ShapeDtypeStruct((m, d), F32)] * 6,
        compiler_params=_cparams(("parallel",)),
        name="rwkv_pre",
    )(x, carry, x_prev_sample, *weights)


def _wkv_kernel(*refs, chunk, nseq):
    streams = [refs[j * nseq:(j + 1) * nseq] for j in range(5)]
    kk_ref, ka_ref, rk_ref, gg_ref, gb_ref, s0_ref, y_ref, sout_ref, s_ref = refs[5 * nseq:]
    c = pl.program_id(1)
    nh = nseq * RWKV_HEADS

    @pl.when(c == 0)
    def _():
        s_ref[...] = s0_ref[...].reshape(nh, RWKV_HEAD, RWKV_HEAD)

    row = lax.broadcasted_iota(jnp.int32, (chunk, chunk), 0)
    col = lax.broadcasted_iota(jnp.int32, (chunk, chunk), 1)
    incl = (col <= row)[None]
    strict = (col < row)[None]
    tri = jnp.where(col <= row, 1.0, 0.0).astype(BF16)
    eye = jnp.where(col == row, 1.0, 0.0).astype(F32)[None]

    def heads(rows_of_seq):
        return jnp.stack([rows_of_seq(i)[:, h * RWKV_HEAD:(h + 1) * RWKV_HEAD]
                          for i in range(nseq) for h in range(RWKV_HEADS)], axis=0)

    def head_param(ref):
        return jnp.stack([ref[:, h * RWKV_HEAD:(h + 1) * RWKV_HEAD]
                          for _ in range(nseq) for h in range(RWKV_HEADS)], axis=0)

    r, k, v, ld, ag = (heads(lambda i, rs=rs: rs[i][...]) for rs in streams)
    lcum = [_dot_exact_lhs(tri, streams[3][i][...]) for i in range(nseq)]
    lc = heads(lambda i: lcum[i])
    kk = k * head_param(kk_ref)
    kk = kk / jnp.maximum(jnp.sqrt(jnp.sum(kk * kk, -1, keepdims=True)), 1e-12)
    kh = k * (1.0 + (ag - 1.0) * head_param(ka_ref))
    b = kk * ag
    lend = lc[:, chunk - 1:chunk, :]
    e_neg = jnp.exp(-lc)
    a_t = -kk * jnp.exp(lc - ld)
    r_t = r * jnp.exp(lc)
    b_t = b * e_neg
    k_t = kh * e_neg
    l_ab = jnp.where(strict, _bmm_nt(a_t, b_t), 0.0)
    l_ak = jnp.where(strict, _bmm_nt(a_t, k_t), 0.0)
    t_rb = jnp.where(incl, _bmm_nt(r_t, b_t), 0.0)
    t_rk = jnp.where(incl, _bmm_nt(r_t, k_t), 0.0)
    s_old = s_ref[...]
    rhs = _bmm_nt(a_t, s_old) + _bmm(l_ak, v)
    inv = eye + l_ab
    lp = l_ab
    n = 2
    while n < chunk:
        lp = _bmm(lp, lp)
        inv = inv + _bmm(inv, lp)
        n *= 2
    u = _bmm(inv, rhs)
    y = _bmm_nt(r_t, s_old) + _bmm(t_rb, u) + _bmm(t_rk, v)
    e_end = jnp.exp(lend - lc)
    uv_t = jnp.swapaxes(jnp.concatenate([u, v], axis=1), 1, 2)
    bke = jnp.concatenate([b * e_end, kh * e_end], axis=1)
    s_ref[...] = s_old * jnp.exp(lend) + _bmm(uv_t, bke)
    mu_y = jnp.mean(y, -1, keepdims=True)
    yc = y - mu_y
    var_y = jnp.mean(yc * yc, -1, keepdims=True)
    yn = yc * lax.rsqrt(var_y + GN_EPS) * head_param(gg_ref) + head_param(gb_ref)
    out = yn + jnp.sum(r * kh * head_param(rk_ref), -1, keepdims=True) * v
    for i in range(nseq):
        for h in range(RWKV_HEADS):
            y_ref[i, :, h * RWKV_HEAD:(h + 1) * RWKV_HEAD] = out[i * RWKV_HEADS + h]

    @pl.when(c == pl.num_programs(1) - 1)
    def _():
        sout_ref[...] = s_ref[...].reshape(sout_ref.shape)


def _wkv(streams, n, t, s0, k_k, k_a, r_k, gn_g, gn_b, chunk, nseq):
    d = streams[0].shape[1]
    nc = t // chunk
    seq_in = lambda s: pl.BlockSpec((chunk, d), lambda i, c: ((i * nseq + s) * nc + c, 0))
    seq = pl.BlockSpec((nseq, chunk, d), lambda i, c: (i, c, 0))
    par = pl.BlockSpec((1, d), lambda i, c: (0, 0))
    st = pl.BlockSpec((nseq, RWKV_HEADS, RWKV_HEAD, RWKV_HEAD), lambda i, c: (i, 0, 0, 0))
    vec = lambda a: a.reshape(1, d)
    r, k, v, ld, ag = ([a] * nseq for a in streams)
    return pl.pallas_call(
        functools.partial(_wkv_kernel, chunk=chunk, nseq=nseq),
        grid=(n // nseq, nc),
        in_specs=[seq_in(s) for _ in range(5) for s in range(nseq)] + [par] * 5 + [st],
        out_specs=[seq, st],
        out_shape=[jax.ShapeDtypeStruct((n, t, d), F32),
                   jax.ShapeDtypeStruct((n, RWKV_HEADS, RWKV_HEAD, RWKV_HEAD), F32)],
        scratch_shapes=[pltpu.VMEM((nseq * RWKV_HEADS, RWKV_HEAD, RWKV_HEAD), F32)],
        compiler_params=_cparams(("parallel", "arbitrary")),
        name="wkv",
    )(*r, *k, *v, *ld, *ag, vec(k_k), vec(k_a), vec(r_k), vec(gn_g), vec(gn_b), s0)


PAGES_PER_SEQ = 16
SEG_PER_PAGE = PAGE_SIZE // CMP_STRIDE
N_SEG = PAGES_PER_SEQ * SEG_PER_PAGE
N_CMP = N_SEG - 1
KV_COLS = KV_HEADS * HEAD_DIM


def _gelu_tanh(x):
    return x * (0.5 * (1.0 + jnp.tanh(np.sqrt(2.0 / np.pi) * (x + 0.044715 * (x * x * x)))))


def _compress_kernel(pt_ref, *refs):
    n_slab = 2 * KV_COLS // LANES
    pages = refs[:PAGES_PER_SEQ * n_slab]
    w1c_ref, w1_ref, pe_ref, b1_ref, w2_ref, b2_ref, kc_ref, vc_ref = refs[PAGES_PER_SEQ * n_slab:]
    valid = lax.broadcasted_iota(jnp.int32, (N_SEG, HEAD_DIM), 0) < N_CMP
    for typ, out_ref in ((0, kc_ref), (1, vc_ref)):
        pe8 = jnp.broadcast_to(pe_ref[typ], (8, CMP_BLOCK * HEAD_DIM))
        c0 = _bdot(pe8, w1_ref[typ])[0:1, :] + b1_ref[typ]
        for pair in range(KV_HEADS // 2):
            cb = typ * (KV_HEADS // 2) + pair
            acc = jnp.zeros((N_SEG, 4 * HEAD_DIM), F32)
            for p in range(CMP_STRIDE):
                xp = jnp.concatenate(
                    [pages[k * n_slab + cb][0, pl.ds(p, SEG_PER_PAGE, stride=CMP_STRIDE), :]
                     for k in range(PAGES_PER_SEQ)], axis=0)
                acc = acc + _bdot(xp, w1c_ref[typ, p])
            for gg in range(2):
                g = pair * 2 + gg
                first = acc[:, gg * LANES:gg * LANES + HEAD_DIM]
                second = acc[:, gg * LANES + HEAD_DIM:(gg + 1) * LANES]
                second_next = pltpu.roll(second, N_SEG - 1, 0)
                hid = _gelu_tanh(first + second_next + c0)
                out = _bdot(hid, w2_ref[typ]) + b2_ref[typ]
                out_ref[0, :, g * HEAD_DIM:(g + 1) * HEAD_DIM] = jnp.where(valid, out, 0.0)


def _compress(pages, layer, page_ids, n_seq, pe, w1, b1, w2, b2):
    w1c = w1.reshape(2, 2, CMP_STRIDE, HEAD_DIM, HEAD_DIM).transpose(0, 2, 3, 1, 4).reshape(
        2, CMP_STRIDE, HEAD_DIM, 2 * HEAD_DIM)
    zero = jnp.zeros_like(w1c)
    w1c = jnp.concatenate([jnp.concatenate([w1c, zero], axis=3), jnp.concatenate([zero, w1c], axis=3)],
                          axis=2)
    n_slab = 2 * KV_COLS // LANES
    slab0 = layer * (4 * KV_COLS // LANES)
    page_spec = lambda k, cb: pl.BlockSpec((1, PAGE_SIZE, LANES),
                                           lambda b, pt: (pt[b * PAGES_PER_SEQ + k], 0, slab0 + cb))
    full = lambda a: pl.BlockSpec(a.shape, lambda b, pt: (0,) * a.ndim)
    consts = (w1c, w1, pe.reshape(2, 1, CMP_BLOCK * HEAD_DIM), b1.reshape(2, 1, HEAD_DIM), w2,
              b2.reshape(2, 1, HEAD_DIM))
    out_spec = pl.BlockSpec((1, N_SEG, KV_COLS), lambda b, pt: (b, 0, 0))
    grid_spec = pltpu.PrefetchScalarGridSpec(
        num_scalar_prefetch=1,
        grid=(n_seq,),
        in_specs=([page_spec(k, cb) for k in range(PAGES_PER_SEQ) for cb in range(n_slab)]
                  + [full(a) for a in consts]),
        out_specs=[out_spec, out_spec],
    )
    return pl.pallas_call(
        _compress_kernel,
        grid_spec=grid_spec,
        out_shape=[jax.ShapeDtypeStruct((n_seq, N_SEG, KV_COLS), F32)] * 2,
        compiler_params=_cparams(("parallel",)),
        name="nsa_compress",
    )(page_ids, *([pages] * (PAGES_PER_SEQ * n_slab)), *consts)


def _overlap_matrix(n_blk):
    c0 = np.arange(N_CMP)[:, None] * CMP_STRIDE
    c1 = c0 + CMP_BLOCK - 1
    s0 = np.arange(n_blk)[None, :] * SEL_BLOCK
    s1 = s0 + SEL_BLOCK - 1
    ov = np.zeros((N_SEG, LANES), np.float32)
    ov[:N_CMP, :n_blk] = (c0 <= s1) & (c1 >= s0)
    return jnp.asarray(ov, BF16)


def _expand_matrix(n_keys):
    ex = (np.arange(LANES)[:, None] == (np.arange(n_keys)[None, :] // SEL_BLOCK)).astype(np.float32)
    return jnp.asarray(ex, BF16)


def _masked_softmax(s, mask):
    s = jnp.where(mask, s, NEG)
    m = jnp.max(s, -1, keepdims=True)
    e = jnp.where(mask, jnp.exp(s - m), 0.0)
    return e / jnp.maximum(jnp.sum(e, -1, keepdims=True), 1e-30)


def _select_blocks(imp, qblk, n_blk):
    lane = lax.broadcasted_iota(jnp.int32, imp.shape, 1)
    forced = (lane == 0) | (lane == qblk) | (lane == qblk - 1)
    imp = jnp.where(forced, jnp.inf, imp)
    imp = jnp.where(lane <= qblk, imp, -jnp.inf)
    rank = jnp.zeros(imp.shape, F32)
    for s in range(n_blk):
        c = imp[:, s:s + 1]
        tie_ahead = jnp.where(lane > s, 1.0, 0.0)
        rank = rank + jnp.where(c > imp, 1.0, jnp.where(c == imp, tie_ahead, 0.0))
    return jnp.where(rank < N_SEL, jnp.where(imp > -jnp.inf, 1.0, 0.0), 0.0)


def _nsa_prompt_kernel(q_ref, gt_ref, kc_ref, vc_ref, ks_ref, vs_ref, kw_ref, vw_ref, ov_ref, ex_ref,
                       o_ref, qs_ref, mx_ref, m_ref, l_ref, acc_ref, oc_ref, *, n_chunks):
    qi = pl.program_id(2)
    scale = HEAD_DIM ** -0.5
    rowq = lax.broadcasted_iota(jnp.int32, (QB, QB), 0)
    colk = lax.broadcasted_iota(jnp.int32, (QB, QB), 1)
    qpos1 = qi * QB + lax.broadcasted_iota(jnp.int32, (QB, 1), 0)
    cmp_end = colk * CMP_STRIDE + (CMP_BLOCK - 1)
    mask_c = (colk < N_CMP) & (cmp_end <= qpos1)

    m_ref[...] = jnp.full(m_ref.shape, NEG, F32)
    l_ref[...] = jnp.zeros(l_ref.shape, F32)
    acc_ref[...] = jnp.zeros(acc_ref.shape, F32)

    for g in range(2):
        for h in range(HPG):
            c0 = (g * HPG + h) * HEAD_DIM
            qs_ref[g, h * QB:(h + 1) * QB, :] = (q_ref[:, c0:c0 + HEAD_DIM] * scale).astype(BF16)
        gs = slice(g * HEAD_DIM, (g + 1) * HEAD_DIM)
        s = _bdot_nt(qs_ref[g], kc_ref[0, :, gs]).reshape(HPG, QB, QB)
        p = _masked_softmax(s, mask_c[None])
        oc_ref[g] = _bdot(p.reshape(HPG * QB, QB), vc_ref[0, :, gs])
        imp = _dot_exact_rhs(p[0] + p[1] + p[2] + p[3], ov_ref[...])
        sel = _select_blocks(imp, qpos1 // SEL_BLOCK, n_chunks * QB // SEL_BLOCK)
        key_mask = jnp.dot(sel.astype(BF16), ex_ref[...], preferred_element_type=F32)
        for c in range(n_chunks):
            mx_ref[g, c] = key_mask[:, c * QB:(c + 1) * QB]

    def online_update(br, g, k, v, mask):
        s = _bdot_nt(qs_ref[g], k).reshape(HPG, QB, QB)
        s = jnp.where(mask[None], s, NEG)
        m_old = m_ref[br, g]
        m_new = jnp.maximum(m_old, jnp.max(s, -1, keepdims=True))
        alpha = jnp.exp(m_old - m_new)
        p = jnp.where(mask[None], jnp.exp(s - m_new), 0.0)
        l_ref[br, g] = alpha * l_ref[br, g] + jnp.sum(p, -1, keepdims=True)
        acc_ref[br, g] = (alpha.reshape(HPG * QB, 1) * acc_ref[br, g]
                          + _bdot(p.reshape(HPG * QB, QB), v))
        m_ref[br, g] = m_new

    def sel_body(kc, carry):
        off = pl.multiple_of(kc * QB, QB)
        causal = (kc - qi) * QB + colk - rowq <= 0
        for g in range(2):
            k = ks_ref[pl.ds(off, QB), pl.ds(g * HEAD_DIM, HEAD_DIM)]
            v = vs_ref[pl.ds(off, QB), pl.ds(g * HEAD_DIM, HEAD_DIM)]
            mask = causal & (mx_ref[g, kc] > 0.5)
            online_update(0, g, k, v, mask)
        return carry

    lax.fori_loop(0, qi + 1, sel_body, 0)

    def win_body(kc, carry):
        off = pl.multiple_of(kc * QB, QB)
        dist = (qi - kc) * QB + rowq - colk
        mask = (dist >= 0) & (dist < WINDOW)
        for g in range(2):
            k = kw_ref[pl.ds(off, QB), pl.ds(g * HEAD_DIM, HEAD_DIM)]
            v = vw_ref[pl.ds(off, QB), pl.ds(g * HEAD_DIM, HEAD_DIM)]
            online_update(1, g, k, v, mask)
        return carry

    lax.fori_loop(jnp.maximum(qi - WINDOW // QB, 0), qi + 1, win_body, 0)

    gt = jax.nn.sigmoid(gt_ref[...])
    for g in range(2):
        o_sel = acc_ref[0, g] / jnp.maximum(l_ref[0, g], 1e-30).reshape(HPG * QB, 1)
        o_win = acc_ref[1, g] / jnp.maximum(l_ref[1, g], 1e-30).reshape(HPG * QB, 1)
        o_cmp = oc_ref[g]
        for h in range(HPG):
            rows = slice(h * QB, (h + 1) * QB)
            gc = (g * HPG + h) * 3
            o = (gt[:, gc:gc + 1] * o_cmp[rows] + gt[:, gc + 1:gc + 2] * o_sel[rows]
                 + gt[:, gc + 2:gc + 3] * o_win[rows])
            c0 = (g * HPG + h) * HEAD_DIM
            o_ref[:, c0:c0 + HEAD_DIM] = o


def _nsa_prompt(q, gates, kc, vc, rows, win, b, t):
    nq = t // QB
    pair = 2 * HEAD_DIM
    blk = lambda shape, fn: pl.BlockSpec(shape, fn)
    seq_cols = lambda cb: blk((t, pair), lambda i, gp, qi: (i, cb + gp))
    return pl.pallas_call(
        functools.partial(_nsa_prompt_kernel, n_chunks=nq),
        grid=(b, KV_HEADS // 2, nq),
        in_specs=[blk((QB, 2 * HPG * HEAD_DIM), lambda i, gp, qi: (i * nq + qi, gp)),
                  blk((QB, LANES), lambda i, gp, qi: (i * nq + qi, gp)),
                  blk((1, N_SEG, pair), lambda i, gp, qi: (i, 0, gp)),
                  blk((1, N_SEG, pair), lambda i, gp, qi: (i, 0, gp)),
                  seq_cols(4), seq_cols(6),
                  seq_cols(0), seq_cols(2),
                  blk((N_SEG, LANES), lambda i, gp, qi: (0, 0)),
                  blk((LANES, t), lambda i, gp, qi: (0, 0))],
        out_specs=blk((QB, 2 * HPG * HEAD_DIM), lambda i, gp, qi: (i * nq + qi, gp)),
        out_shape=jax.ShapeDtypeStruct((b * t, N_HEADS * HEAD_DIM), F32),
        scratch_shapes=[pltpu.VMEM((2, HPG * QB, HEAD_DIM), BF16),
                        pltpu.VMEM((2, nq, QB, QB), F32),
                        pltpu.VMEM((2, 2, HPG, QB, 1), F32),
                        pltpu.VMEM((2, 2, HPG, QB, 1), F32),
                        pltpu.VMEM((2, 2, HPG * QB, HEAD_DIM), F32),
                        pltpu.VMEM((2, HPG * QB, HEAD_DIM), F32)],
        compiler_params=_cparams(("parallel", "parallel", "arbitrary")),
        name="nsa_prompt",
    )(q, gates, kc, vc, rows, rows, win, win, _overlap_matrix(t // SEL_BLOCK), _expand_matrix(t))


TQ = 8


def _nsa_sample_kernel(pt_ref, *refs, past_len, n_new):
    pages = refs[:PAGES_PER_SEQ]
    q_ref, gt_ref, kc_ref, vc_ref, new_ref, wb_ref, wn_ref, ov_ref, ex_ref, o_ref = refs[PAGES_PER_SEQ:]
    scale = HEAD_DIM ** -0.5
    rows = HPG * TQ
    tok1 = lax.broadcasted_iota(jnp.int32, (TQ, 1), 0)
    qpos1 = past_len + tok1
    lane = lax.broadcasted_iota(jnp.int32, (TQ, LANES), 1)
    mask_c = (lane < N_CMP) & (lane * CMP_STRIDE + (CMP_BLOCK - 1) <= qpos1)
    newj = lax.broadcasted_iota(jnp.int32, (TQ, TQ), 1)
    newt = lax.broadcasted_iota(jnp.int32, (TQ, TQ), 0)
    mask_new = (newj < n_new) & (newj <= newt)
    wcache = wb_ref.shape[1]
    wi = lax.broadcasted_iota(jnp.int32, (TQ, wcache), 1)
    wpos = past_len - wcache + wi
    mask_w = (wpos <= qpos1) & (wpos > qpos1 - WINDOW)
    n_blk = -(-(past_len + n_new) // SEL_BLOCK)
    gt = jax.nn.sigmoid(gt_ref[0])

    def bcast(mask):
        return jnp.broadcast_to(mask[None], (HPG,) + mask.shape)

    def two_part_attention(s_a, mask_a, s_b, mask_b):
        s_a = jnp.where(mask_a, s_a, NEG)
        s_b = jnp.where(mask_b, s_b, NEG)
        m = jnp.maximum(jnp.max(s_a, -1, keepdims=True), jnp.max(s_b, -1, keepdims=True))
        e_a = jnp.where(mask_a, jnp.exp(s_a - m), 0.0)
        e_b = jnp.where(mask_b, jnp.exp(s_b - m), 0.0)
        den = jnp.maximum(jnp.sum(e_a, -1, keepdims=True) + jnp.sum(e_b, -1, keepdims=True), 1e-30)
        return e_a / den, e_b / den

    for g in range(KV_HEADS):
        gs = slice(g * HEAD_DIM, (g + 1) * HEAD_DIM)
        vs_cols = slice(KV_COLS + g * HEAD_DIM, KV_COLS + (g + 1) * HEAD_DIM)
        qg = jnp.concatenate(
            [q_ref[0, :, (g * HPG + h) * HEAD_DIM:(g * HPG + h + 1) * HEAD_DIM] for h in range(HPG)],
            axis=0) * scale
        qg = qg.astype(BF16)
        s = _bdot_nt(qg, kc_ref[0, :, gs]).reshape(HPG, TQ, N_SEG)
        p = _masked_softmax(s, mask_c[None])
        o_cmp = _bdot(p.reshape(rows, N_SEG), vc_ref[0, :, gs])
        imp = _dot_exact_rhs(p[0] + p[1] + p[2] + p[3], ov_ref[...])
        sel = _select_blocks(imp, qpos1 // SEL_BLOCK, n_blk)
        past_mask = jnp.dot(sel.astype(BF16), ex_ref[...], preferred_element_type=F32) > 0.5
        new_blk = past_len // SEL_BLOCK
        mask_sn = mask_new & (sel[:, new_blk:new_blk + 1] > 0.5)
        s_past = jnp.concatenate([_bdot_nt(qg, pg[0, :, gs]) for pg in pages], axis=1)
        s_new = _bdot_nt(qg, new_ref[0, :, 2 * KV_COLS + g * HEAD_DIM:2 * KV_COLS + (g + 1) * HEAD_DIM])
        p_past, p_new = two_part_attention(s_past.reshape(HPG, TQ, -1), bcast(past_mask),
                                           s_new.reshape(HPG, TQ, TQ), bcast(mask_sn))
        p_past = p_past.reshape(rows, -1)
        o_sel = _bdot(p_new.reshape(rows, TQ),
                      new_ref[0, :, 3 * KV_COLS + g * HEAD_DIM:3 * KV_COLS + (g + 1) * HEAD_DIM])
        for k, pg in enumerate(pages):
            o_sel = o_sel + _bdot(p_past[:, k * PAGE_SIZE:(k + 1) * PAGE_SIZE], pg[0, :, vs_cols])
        s_wb = _bdot_nt(qg, wb_ref[0, :, gs])
        s_wn = _bdot_nt(qg, wn_ref[0, :, gs])
        p_wb, p_wn = two_part_attention(s_wb.reshape(HPG, TQ, wcache), bcast(mask_w),
                                        s_wn.reshape(HPG, TQ, TQ), bcast(mask_new))
        o_win = (_bdot(p_wb.reshape(rows, wcache), wb_ref[0, :, vs_cols])
                 + _bdot(p_wn.reshape(rows, TQ), wn_ref[0, :, vs_cols]))
        for h in range(HPG):
            r8 = slice(h * TQ, (h + 1) * TQ)
            gc = (g // 2) * LANES + ((g % 2) * HPG + h) * 3
            o = (gt[:, gc:gc + 1] * o_cmp[r8] + gt[:, gc + 1:gc + 2] * o_sel[r8]
                 + gt[:, gc + 2:gc + 3] * o_win[r8])
            c0 = (g * HPG + h) * HEAD_DIM
            o_ref[0, :, c0:c0 + HEAD_DIM] = o


def _nsa_sample(pages, layer, page_ids, q, gates, kc, vc, new_rows, win_buf, win_new, past_len, n_new):
    n = q.shape[0]
    wc = win_buf.shape[1]
    n_blk = -(-(past_len + n_new) // SEL_BLOCK)
    page_spec = lambda k: pl.BlockSpec((1, PAGE_SIZE, 2 * KV_COLS),
                                       lambda b, pt: (pt[b * PAGES_PER_SEQ + k], 0, 2 * layer + 1))
    seq = lambda a: pl.BlockSpec((1,) + a.shape[1:], lambda b, pt: (b,) + (0,) * (a.ndim - 1))
    full = lambda a: pl.BlockSpec(a.shape, lambda b, pt: (0,) * a.ndim)
    ov, ex = _overlap_matrix(n_blk), _expand_matrix(past_len)
    per_seq = (q, gates, kc, vc, new_rows, win_buf, win_new)
    grid_spec = pltpu.PrefetchScalarGridSpec(
        num_scalar_prefetch=1,
        grid=(n,),
        in_specs=([page_spec(k) for k in range(PAGES_PER_SEQ)] + [seq(a) for a in per_seq]
                  + [full(ov), full(ex)]),
        out_specs=pl.BlockSpec((1, TQ, N_HEADS * HEAD_DIM), lambda b, pt: (b, 0, 0)),
    )
    return pl.pallas_call(
        functools.partial(_nsa_sample_kernel, past_len=past_len, n_new=n_new),
        grid_spec=grid_spec,
        out_shape=jax.ShapeDtypeStruct((n, TQ, N_HEADS * HEAD_DIM), F32),
        compiler_params=_cparams(("parallel",)),
        name="nsa_sample",
    )(page_ids, *([pages] * PAGES_PER_SEQ), *per_seq, ov, ex)


WKV_SAMPLE_CHUNK = 16
WKV_PROMPT_SEQS = 2
WKV_SAMPLE_SEQS = 8


def _pad_rows(a, n):
    return jnp.pad(a, ((0, 0), (0, n - a.shape[1]), (0, 0)))


def _nsa_layer(x, n_prompt, b, t, db, ds, pages, layer, page_table, win_cache, w_in, w_out, cmp, ln_g,
               ln_b):
    d = x.shape[1]
    qd = N_HEADS * HEAD_DIM
    q = _proj(x, w_in, 512, 0, qd)
    rows = _proj(x, w_in, 512, 2, 4 * KV_COLS)
    win = _proj(x, w_in, 512, 4, 2 * KV_COLS)
    n_gate = 2 * HPG * 3
    g0 = qd + 6 * KV_COLS
    wg = jnp.concatenate(
        [jnp.pad(w_in[:, g0 + p * n_gate:g0 + (p + 1) * n_gate], ((0, 0), (0, LANES - n_gate)))
         for p in range(KV_HEADS // 2)], axis=1)
    gates = _proj(x, wg, wg.shape[1], 0, wg.shape[1])
    past_len = page_table.shape[1] * PAGE_SIZE

    prompt_pages = rows.reshape(rows.shape[0] // PAGE_SIZE, PAGE_SIZE, 4 * KV_COLS)
    kc_p, vc_p = _compress(prompt_pages, 0, jnp.arange(b * PAGES_PER_SEQ, dtype=jnp.int32), b, *cmp)
    o_p = _nsa_prompt(q, gates, kc_p, vc_p, rows, win, b, t)

    q_s, rows_s, win_s, gates_s = (a[n_prompt:].reshape(db, ds, -1) for a in (q, rows, win, gates))
    page_ids = page_table.reshape(-1).astype(jnp.int32)
    kc_s, vc_s = _compress(pages, layer, page_ids, db, *cmp)
    win_buf = win_cache.reshape(db, win_cache.shape[1], 2 * KV_COLS)
    o_s = _nsa_sample(pages, layer, page_ids, _pad_rows(q_s, TQ), _pad_rows(gates_s, TQ), kc_s, vc_s,
                      _pad_rows(rows_s, TQ), win_buf, _pad_rows(win_s, TQ), past_len, ds)[:, :ds]

    o = jnp.concatenate([o_p, o_s.reshape(db * ds, qd)], axis=0)
    x_new = _out_ln(o, None, w_out, x, ln_g, ln_b)
    keep = min(WINDOW, t)
    win_p = win[:n_prompt].reshape(b, t, 2 * KV_COLS)
    outs = (rows[:n_prompt].reshape(b, t, 4, KV_HEADS, HEAD_DIM),
            win_p[:, t - keep:].reshape(b, keep, 2, KV_HEADS, HEAD_DIM),
            rows_s.reshape(db, ds, 4, KV_HEADS, HEAD_DIM),
            jnp.concatenate([win_buf, win_s], axis=1)[:, ds:].reshape(db, -1, 2, KV_HEADS, HEAD_DIM))
    return x_new, outs


def _rwkv_layer(x, n_prompt, b, t, db, ds, shift_s, state_s, mu, w_rkv, w0, w1, w2, a0, a1, a2, g1, g2,
                k_k, k_a, r_k, gn_g, gn_b, w_out, ln_g, ln_b):
    d = x.shape[1]
    xs = x[n_prompt:].reshape(db, ds, d)
    prev_s = jnp.concatenate([shift_s[:, None].astype(x.dtype), xs[:, :-1]], axis=1).reshape(db * ds, d)
    tiles, p_tiles = x.shape[0] // ROW_TILE, n_prompt // ROW_TILE
    last_rows = x[ROW_TILE - 1:n_prompt:ROW_TILE]
    carry = jnp.concatenate([jnp.zeros((1, d), x.dtype), last_rows[:-1]], axis=0)
    seq_start = (jnp.arange(p_tiles) * ROW_TILE) % t == 0
    carry = jnp.where(seq_start[:, None], 0.0, carry)
    carry = jnp.pad(carry, ((0, tiles - p_tiles), (0, 0))).reshape(tiles, 1, d)
    r, k, v, ld, ag, g = _rwkv_pre(x, carry, prev_s, mu, w_rkv, w0, w1, w2, a0, a1, a2, g1, g2)
    streams = (r, k, v, ld, ag)
    head_params = (k_k, k_a, r_k.reshape(d), gn_g, gn_b)
    zero_state = jnp.zeros((b, RWKV_HEADS, RWKV_HEAD, RWKV_HEAD), F32)
    y_p, wkv_p = _wkv(streams, b, t, zero_state, *head_params, WKV_CHUNK, WKV_PROMPT_SEQS)
    streams_s = [_pad_rows(a[n_prompt:].reshape(db, ds, d), WKV_SAMPLE_CHUNK).reshape(-1, d)
                 for a in streams]
    y_s, wkv_s = _wkv(streams_s, db, WKV_SAMPLE_CHUNK, state_s.astype(F32), *head_params,
                      WKV_SAMPLE_CHUNK, WKV_SAMPLE_SEQS)
    y = jnp.concatenate([y_p.reshape(n_prompt, d), y_s[:, :ds].reshape(db * ds, d)], axis=0)
    x_new = _out_ln(y, g, w_out, x, ln_g, ln_b)
    return x_new, (wkv_p, x[t - 1:n_prompt:t], wkv_s.astype(state_s.dtype), xs[:, -1])


def kernel(x_prompt, x_sample, cache_nsa_paged, cache_nsa_win, state_rwkv_wkv, state_rwkv_shift, page_table, nsa_w_in, nsa_w_out, nsa_cmp_pe, nsa_cmp_w1, nsa_cmp_b1, nsa_cmp_w2, nsa_cmp_b2, rwkv_mu, rwkv_w_rkv, rwkv_w0, rwkv_w1, rwkv_w2, rwkv_a0, rwkv_a1, rwkv_a2, rwkv_g1, rwkv_g2, rwkv_k_k, rwkv_k_a, rwkv_r_k, rwkv_gn_g, rwkv_gn_b, rwkv_w_out, ln_g, ln_b, moe_w_router, moe_b_router, moe_w_up, moe_b_up, moe_w_down, moe_b_down):
    b, t, d = x_prompt.shape
    db, ds, _ = x_sample.shape
    n_prompt = b * t
    x = jnp.concatenate([x_prompt.reshape(n_prompt, d), x_sample.reshape(db * ds, d)], axis=0)
    n_phys = cache_nsa_paged.shape[0]
    nsa_outs, rwkv_outs = [], []
    for i in range(DEPTH):
        j = i // 2
        if i % 2 == 0:
            pages = cache_nsa_paged.reshape(n_phys, PAGE_SIZE, -1)
            cmp = (nsa_cmp_pe[j], nsa_cmp_w1[j], nsa_cmp_b1[j], nsa_cmp_w2[j], nsa_cmp_b2[j])
            x, outs = _nsa_layer(x, n_prompt, b, t, db, ds, pages, j, page_table, cache_nsa_win[j],
                                 nsa_w_in[j], nsa_w_out[j], cmp, ln_g[i, 0], ln_b[i, 0])
            nsa_outs.append(outs)
        else:
            x, outs = _rwkv_layer(x, n_prompt, b, t, db, ds, state_rwkv_shift[j], state_rwkv_wkv[j],
                                  rwkv_mu[j], rwkv_w_rkv[j], rwkv_w0[j], rwkv_w1[j], rwkv_w2[j],
                                  rwkv_a0[j], rwkv_a1[j], rwkv_a2[j], rwkv_g1[j], rwkv_g2[j],
                                  rwkv_k_k[j], rwkv_k_a[j], rwkv_r_k[j], rwkv_gn_g[j], rwkv_gn_b[j],
                                  rwkv_w_out[j], ln_g[i, 0], ln_b[i, 0])
            rwkv_outs.append(outs)
        x = _moe_layer(x, moe_w_router[i], moe_b_router[i], moe_w_up[i], moe_b_up[i], moe_w_down[i],
                       moe_b_down[i], ln_g[i, 1], ln_b[i, 1])
    p_rows, p_win, s_rows, s_win = (jnp.stack(z, axis=k) for z, k in zip(zip(*nsa_outs), (2, 0, 2, 0)))
    p_wkv, p_shift, s_wkv, s_shift = (jnp.stack(z, axis=0) for z in zip(*rwkv_outs))
    return (x[:n_prompt].reshape(b, t, d), x[n_prompt:].reshape(db, ds, d),
            p_rows, p_win, p_wkv, p_shift, s_rows, s_win, s_wkv, s_shift)
```

```python
import functools

import numpy as np
import jax
import jax.numpy as jnp
from jax import lax
from jax.experimental import pallas as pl
from jax.experimental.pallas import tpu as pltpu

F32 = jnp.float32
BF16 = jnp.bfloat16

D_MODEL = 1024
DEPTH = 2
N_HEADS = 16
HEAD_DIM = 64
KV_HEADS = 4
HPG = N_HEADS // KV_HEADS
CMP_BLOCK = 32
CMP_STRIDE = 16
SEL_BLOCK = 64
N_SEL = 16
WINDOW = 512
PAGE_SIZE = 128
RWKV_HEAD = 64
RWKV_HEADS = D_MODEL // RWKV_HEAD
GN_EPS = 64e-5
N_EXPERTS = 32
TOP_K = 4
D_EXPERT = D_MODEL
SWIGLU_ALPHA = 1.702
SWIGLU_LIMIT = 7.0
LN_EPS = 1e-5
DN_ALPHA = (2 * DEPTH) ** 0.25

LANES = 128
QB = 128
NEG = -1e30
VMEM_LIMIT = 56 * 1024 * 1024
ROW_TILE = 512
MOE_ROWS = 256
WKV_CHUNK = 64


def _cparams(sem):
    return pltpu.CompilerParams(dimension_semantics=sem, vmem_limit_bytes=VMEM_LIMIT)


def _bdot(a, b):
    return jnp.dot(a.astype(BF16), b.astype(BF16), preferred_element_type=F32)


def _bdot_nt(a, b):
    return lax.dot_general(a.astype(BF16), b.astype(BF16), (((1,), (1,)), ((), ())),
                           preferred_element_type=F32)


def _bdot_tn(a, b):
    return lax.dot_general(a.astype(BF16), b.astype(BF16), (((0,), (0,)), ((), ())),
                           preferred_element_type=F32)


def _bmm(a, b):
    return lax.dot_general(a.astype(BF16), b.astype(BF16), (((2,), (1,)), ((0,), (0,))),
                           preferred_element_type=F32)


def _bmm_nt(a, b):
    return lax.dot_general(a.astype(BF16), b.astype(BF16), (((2,), (2,)), ((0,), (0,))),
                           preferred_element_type=F32)


def _split3(x):
    hi = x.astype(BF16)
    r = x - hi.astype(F32)
    mid = r.astype(BF16)
    lo = (r - mid.astype(F32)).astype(BF16)
    return hi, mid, lo


def _dot_exact_rhs(x, w01):
    hi, mid, lo = _split3(x)
    d = lambda a: jnp.dot(a, w01, preferred_element_type=F32)
    return d(hi) + d(mid) + d(lo)


def _dot_exact_lhs(w01, x):
    hi, mid, lo = _split3(x)
    d = lambda a: jnp.dot(w01, a, preferred_element_type=F32)
    return d(hi) + d(mid) + d(lo)


def _layer_norm(z, g, b):
    mu = jnp.mean(z, -1, keepdims=True)
    zc = z - mu
    var = jnp.mean(zc * zc, -1, keepdims=True)
    return zc * lax.rsqrt(var + LN_EPS) * g + b


def _proj_kernel(x_ref, w_ref, o_ref):
    o_ref[...] = _bdot(x_ref[...], w_ref[...])


def _proj(x, w, tn, col_block0, n_out):
    m, k = x.shape
    return pl.pallas_call(
        _proj_kernel,
        grid=(n_out // tn, m // ROW_TILE),
        in_specs=[pl.BlockSpec((ROW_TILE, k), lambda j, i: (i, 0)),
                  pl.BlockSpec((k, tn), lambda j, i: (0, j + col_block0))],
        out_specs=pl.BlockSpec((ROW_TILE, tn), lambda j, i: (i, j)),
        out_shape=jax.ShapeDtypeStruct((m, n_out), F32),
        compiler_params=_cparams(("parallel", "parallel")),
        name="proj",
    )(x, w)


def _out_ln_kernel(*refs, gated):
    if gated:
        a_ref, m_ref, w_ref, res_ref, g_ref, b_ref, o_ref = refs
        a = a_ref[...] * m_ref[...]
    else:
        a_ref, w_ref, res_ref, g_ref, b_ref, o_ref = refs
        a = a_ref[...]
    z = DN_ALPHA * res_ref[...] + _bdot(a, w_ref[...])
    o_ref[...] = _layer_norm(z, g_ref[...], b_ref[...])


def _out_ln(a, mul, w, res, g, b):
    m, k = a.shape
    n = w.shape[1]
    row = lambda i: (i, 0)
    fixed = lambda i: (0, 0)
    ins = [a] + ([mul] if mul is not None else []) + [w, res, g.reshape(1, n), b.reshape(1, n)]
    specs = ([pl.BlockSpec((ROW_TILE, k), row)] * (2 if mul is not None else 1)
             + [pl.BlockSpec((k, n), fixed), pl.BlockSpec((ROW_TILE, n), row),
                pl.BlockSpec((1, n), fixed), pl.BlockSpec((1, n), fixed)])
    return pl.pallas_call(
        functools.partial(_out_ln_kernel, gated=mul is not None),
        grid=(m // ROW_TILE,),
        in_specs=specs,
        out_specs=pl.BlockSpec((ROW_TILE, n), row),
        out_shape=jax.ShapeDtypeStruct((m, n), F32),
        compiler_params=_cparams(("parallel",)),
        name="out_ln",
    )(*ins)


def _router_kernel(x_ref, w_ref, b_ref, o_ref):
    xh, xm, xl = _split3(x_ref[...])
    wh, wm, wl = _split3(w_ref[...])
    d = lambda a, c: jnp.dot(a, c, preferred_element_type=F32)
    acc = d(xh, wh) + (d(xh, wm) + d(xm, wh)) + (d(xm, wm) + d(xh, wl) + d(xl, wh))
    o_ref[...] = acc + b_ref[...]


def _router(x, w_pad, b_pad):
    m, k = x.shape
    return pl.pallas_call(
        _router_kernel,
        grid=(m // ROW_TILE,),
        in_specs=[pl.BlockSpec((ROW_TILE, k), lambda i: (i, 0)),
                  pl.BlockSpec((k, LANES), lambda i: (0, 0)),
                  pl.BlockSpec((1, LANES), lambda i: (0, 0))],
        out_specs=pl.BlockSpec((ROW_TILE, LANES), lambda i: (i, 0)),
        out_shape=jax.ShapeDtypeStruct((m, LANES), F32),
        compiler_params=_cparams(("parallel",)),
        name="router",
    )(x, w_pad, b_pad)


def _moe_kernel(be_ref, nb_ref, x_ref, wu_ref, bu_ref, wd_ref, bd_ref, gw_ref, o_ref):
    i = pl.program_id(0)

    @pl.when(i < nb_ref[0])
    def _():
        u = _bdot(x_ref[...], wu_ref[0]) + bu_ref[0]
        glu = jnp.minimum(u[:, :D_EXPERT], SWIGLU_LIMIT)
        lin = jnp.clip(u[:, D_EXPERT:], -SWIGLU_LIMIT, SWIGLU_LIMIT)
        h = glu * jax.nn.sigmoid(SWIGLU_ALPHA * glu) * (lin + 1.0)
        y = _bdot(h, wd_ref[0]) + bd_ref[0]
        o_ref[...] = y * gw_ref[...]

    @pl.when(i >= nb_ref[0])
    def _():
        o_ref[...] = jnp.zeros_like(o_ref)


def _moe_experts(xg, gw, block_e, n_used, w_up, b_up, w_down, b_down):
    r, d = xg.shape
    nblk = r // MOE_ROWS
    e, _, d2 = w_up.shape
    grid_spec = pltpu.PrefetchScalarGridSpec(
        num_scalar_prefetch=2,
        grid=(nblk,),
        in_specs=[pl.BlockSpec((MOE_ROWS, d), lambda i, be, nb: (i, 0)),
                  pl.BlockSpec((1, d, d2), lambda i, be, nb: (be[i], 0, 0)),
                  pl.BlockSpec((1, 1, d2), lambda i, be, nb: (be[i], 0, 0)),
                  pl.BlockSpec((1, d2 // 2, d), lambda i, be, nb: (be[i], 0, 0)),
                  pl.BlockSpec((1, 1, d), lambda i, be, nb: (be[i], 0, 0)),
                  pl.BlockSpec((MOE_ROWS, 1), lambda i, be, nb: (i, 0))],
        out_specs=pl.BlockSpec((MOE_ROWS, d), lambda i, be, nb: (i, 0)),
    )
    return pl.pallas_call(
        _moe_kernel,
        grid_spec=grid_spec,
        out_shape=jax.ShapeDtypeStruct((r, d), F32),
        compiler_params=_cparams(("arbitrary",)),
        name="moe_experts",
    )(block_e, n_used, xg, w_up, b_up.reshape(e, 1, d2), w_down, b_down.reshape(e, 1, d), gw)


def _combine_ln_kernel(res_ref, y_ref, g_ref, b_ref, o_ref):
    y = y_ref[0] + y_ref[1] + y_ref[2] + y_ref[3]
    o_ref[...] = _layer_norm(DN_ALPHA * res_ref[...] + y, g_ref[...], b_ref[...])


def _combine_ln(res, y4, g, b):
    m, n = res.shape
    return pl.pallas_call(
        _combine_ln_kernel,
        grid=(m // ROW_TILE,),
        in_specs=[pl.BlockSpec((ROW_TILE, n), lambda i: (i, 0)),
                  pl.BlockSpec((TOP_K, ROW_TILE, n), lambda i: (0, i, 0)),
                  pl.BlockSpec((1, n), lambda i: (0, 0)),
                  pl.BlockSpec((1, n), lambda i: (0, 0))],
        out_specs=pl.BlockSpec((ROW_TILE, n), lambda i: (i, 0)),
        out_shape=jax.ShapeDtypeStruct((m, n), F32),
        compiler_params=_cparams(("parallel",)),
        name="combine_ln",
    )(res, y4, g.reshape(1, n), b.reshape(1, n))


def _moe_layer(x, w_router, b_router, w_up, b_up, w_down, b_down, ln_g, ln_b):
    n = x.shape[0]
    wr = jnp.pad(w_router, ((0, 0), (0, LANES - N_EXPERTS)))
    br = jnp.pad(b_router, (0, LANES - N_EXPERTS)).reshape(1, LANES)
    logits = _router(x, wr, br)[:, :N_EXPERTS]
    top_v, top_e = lax.top_k(logits, TOP_K)
    gate = jax.nn.softmax(top_v, axis=-1)
    n_assign = n * TOP_K
    flat_e = top_e.reshape(-1).astype(jnp.int32)
    iota = jnp.arange(n_assign, dtype=jnp.int32)
    e_sorted, order = lax.sort((flat_e, iota), num_keys=1)
    _, sorted_pos = lax.sort((order, iota), num_keys=1)
    experts = jnp.arange(N_EXPERTS, dtype=jnp.int32)
    counts = jnp.sum((flat_e[None, :] == experts[:, None]).astype(jnp.int32), axis=1)
    raw_start = jnp.cumsum(counts) - counts
    padded = (counts + MOE_ROWS - 1) // MOE_ROWS * MOE_ROWS
    pad_end = jnp.cumsum(padded)
    pad_start = pad_end - padded
    n_blocks = -(-n_assign // MOE_ROWS) + N_EXPERTS
    n_rows = n_blocks * MOE_ROWS
    block_start = jnp.arange(n_blocks, dtype=jnp.int32) * MOE_ROWS
    block_e = jnp.minimum(jnp.sum(pad_end[None, :] <= block_start[:, None], axis=1),
                          N_EXPERTS - 1).astype(jnp.int32)
    n_used = (pad_end[-1:] // MOE_ROWS).astype(jnp.int32)
    j = (block_start - pad_start[block_e])[:, None] + jnp.arange(MOE_ROWS, dtype=jnp.int32)[None, :]
    valid = (j < counts[block_e][:, None]).reshape(-1)
    src = jnp.clip(raw_start[block_e][:, None] + j, 0, n_assign - 1).reshape(-1)
    assign_of_row = order[src]
    tok_of_row = jnp.where(valid, assign_of_row // TOP_K, 0)
    gate_of_row = jnp.where(valid, gate.reshape(-1)[assign_of_row], 0.0)
    row_of_assign = (pad_start - raw_start)[flat_e] + sorted_pos
    xg = x[tok_of_row]
    yb = _moe_experts(xg, gate_of_row.reshape(n_rows, 1), block_e, n_used, w_up, b_up, w_down, b_down)
    y4 = yb[row_of_assign.reshape(n, TOP_K).T]
    return _combine_ln(x, y4, ln_g, ln_b)


def _rwkv_pre_kernel(x_ref, carry_ref, xps_ref, mu_ref, wrkv_ref, w0_ref, w1_ref, w2_ref, a0_ref, a1_ref,
                     a2_ref, g1_ref, g2_ref, r_ref, k_ref, v_ref, ld_ref, ag_ref, g_ref, *, prompt_tiles):
    x = x_ref[...]
    first = lax.broadcasted_iota(jnp.int32, x.shape, 0) == 0
    x_prev = jnp.where(first, carry_ref[0], pltpu.roll(x, 1, 0))
    x_prev = jnp.where(pl.program_id(0) >= prompt_tiles, xps_ref[...], x_prev)
    dx = x_prev - x
    mix = lambda s: x + dx * mu_ref[s:s + 1, :]
    r_ref[...] = _bdot(mix(0), wrkv_ref[0])
    k_ref[...] = _bdot(mix(1), wrkv_ref[1])
    v_ref[...] = _bdot(mix(2), wrkv_ref[2])
    lw = w0_ref[...] + _bdot(jnp.tanh(_bdot(mix(3), w1_ref[...])), w2_ref[...])
    z = -lw
    softplus = jnp.maximum(z, 0.0) + jnp.log1p(jnp.exp(-jnp.abs(z)))
    ld_ref[...] = -jnp.exp(-softplus - 0.5)
    ag_ref[...] = jax.nn.sigmoid(a0_ref[...] + _bdot(_bdot(mix(4), a1_ref[...]), a2_ref[...]))
    g_ref[...] = _bdot(jax.nn.sigmoid(_bdot(mix(5), g1_ref[...])), g2_ref[...])


def _rwkv_pre(x, carry, x_prev_sample, mu, w_rkv, w0, w1, w2, a0, a1, a2, g1, g2):
    m, d = x.shape
    prompt_tiles = (m - x_prev_sample.shape[0]) // ROW_TILE
    row = pl.BlockSpec((ROW_TILE, d), lambda i: (i, 0))
    full = lambda a: pl.BlockSpec(a.shape, lambda i: (0,) * a.ndim)
    w0, a0 = w0.reshape(1, d), a0.reshape(1, d)
    weights = (mu, w_rkv, w0, w1, w2, a0, a1, a2, g1, g2)
    return pl.pallas_call(
        functools.partial(_rwkv_pre_kernel, prompt_tiles=prompt_tiles),
        grid=(m // ROW_TILE,),
        in_specs=[row, pl.BlockSpec((1, 1, d), lambda i: (i, 0, 0)),
                  pl.BlockSpec((ROW_TILE, d), lambda i: (jnp.maximum(i - prompt_tiles, 0), 0))]
                 + [full(a) for a in weights],
        out_specs=[row] * 6,
        out_shape=[jax.ShapeDtypeStruct((m, d), F32)] * 6,
        compiler_params=_cparams(("parallel",)),
        name="rwkv_pre",
    )(x, carry, x_prev_sample, *weights)


def _wkv_kernel(*refs, chunk, nseq):
    streams = [refs[j * nseq:(j + 1) * nseq] for j in range(5)]
    kk_ref, ka_ref, rk_ref, gg_ref, gb_ref, s0_ref, y_ref, sout_ref, s_ref = refs[5 * nseq:]
    c = pl.program_id(1)
    nh = nseq * RWKV_HEADS

    @pl.when(c == 0)
    def _():
        s_ref[...] = s0_ref[...].reshape(nh, RWKV_HEAD, RWKV_HEAD)

    row = lax.broadcasted_iota(jnp.int32, (chunk, chunk), 0)
    col = lax.broadcasted_iota(jnp.int32, (chunk, chunk), 1)
    incl = (col <= row)[None]
    strict = (col < row)[None]
    tri = jnp.where(col <= row, 1.0, 0.0).astype(BF16)
    eye = jnp.where(col == row, 1.0, 0.0).astype(F32)[None]

    def heads(rows_of_seq):
        return jnp.stack([rows_of_seq(i)[:, h * RWKV_HEAD:(h + 1) * RWKV_HEAD]
                          for i in range(nseq) for h in range(RWKV_HEADS)], axis=0)

    def head_param(ref):
        return jnp.stack([ref[:, h * RWKV_HEAD:(h + 1) * RWKV_HEAD]
                          for _ in range(nseq) for h in range(RWKV_HEADS)], axis=0)

    r, k, v, ld, ag = (heads(lambda i, rs=rs: rs[i][...]) for rs in streams)
    lcum = [_dot_exact_lhs(tri, streams[3][i][...]) for i in range(nseq)]
    lc = heads(lambda i: lcum[i])
    kk = k * head_param(kk_ref)
    kk = kk / jnp.maximum(jnp.sqrt(jnp.sum(kk * kk, -1, keepdims=True)), 1e-12)
    kh = k * (1.0 + (ag - 1.0) * head_param(ka_ref))
    b = kk * ag
    lend = lc[:, chunk - 1:chunk, :]
    e_neg = jnp.exp(-lc)
    a_t = -kk * jnp.exp(lc - ld)
    r_t = r * jnp.exp(lc)
    b_t = b * e_neg
    k_t = kh * e_neg
    l_ab = jnp.where(strict, _bmm_nt(a_t, b_t), 0.0)
    l_ak = jnp.where(strict, _bmm_nt(a_t, k_t), 0.0)
    t_rb = jnp.where(incl, _bmm_nt(r_t, b_t), 0.0)
    t_rk = jnp.where(incl, _bmm_nt(r_t, k_t), 0.0)
    s_old = s_ref[...]
    rhs = _bmm_nt(a_t, s_old) + _bmm(l_ak, v)
    inv = eye + l_ab
    lp = l_ab
    n = 2
    while n < chunk:
        lp = _bmm(lp, lp)
        inv = inv + _bmm(inv, lp)
        n *= 2
    u = _bmm(inv, rhs)
    y = _bmm_nt(r_t, s_old) + _bmm(t_rb, u) + _bmm(t_rk, v)
    e_end = jnp.exp(lend - lc)
    uv_t = jnp.swapaxes(jnp.concatenate([u, v], axis=1), 1, 2)
    bke = jnp.concatenate([b * e_end, kh * e_end], axis=1)
    s_ref[...] = s_old * jnp.exp(lend) + _bmm(uv_t, bke)
    mu_y = jnp.mean(y, -1, keepdims=True)
    yc = y - mu_y
    var_y = jnp.mean(yc * yc, -1, keepdims=True)
    yn = yc * lax.rsqrt(var_y + GN_EPS) * head_param(gg_ref) + head_param(gb_ref)
    out = yn + jnp.sum(r * kh * head_param(rk_ref), -1, keepdims=True) * v
    for i in range(nseq):
        for h in range(RWKV_HEADS):
            y_ref[i, :, h * RWKV_HEAD:(h + 1) * RWKV_HEAD] = out[i * RWKV_HEADS + h]

    @pl.when(c == pl.num_programs(1) - 1)
    def _():
        sout_ref[...] = s_ref[...].reshape(sout_ref.shape)


def _wkv(streams, n, t, s0, k_k, k_a, r_k, gn_g, gn_b, chunk, nseq):
    d = streams[0].shape[1]
    nc = t // chunk
    seq_in = lambda s: pl.BlockSpec((chunk, d), lambda i, c: ((i * nseq + s) * nc + c, 0))
    seq = pl.BlockSpec((nseq, chunk, d), lambda i, c: (i, c, 0))
    par = pl.BlockSpec((1, d), lambda i, c: (0, 0))
    st = pl.BlockSpec((nseq, RWKV_HEADS, RWKV_HEAD, RWKV_HEAD), lambda i, c: (i, 0, 0, 0))
    vec = lambda a: a.reshape(1, d)
    r, k, v, ld, ag = ([a] * nseq for a in streams)
    return pl.pallas_call(
        functools.partial(_wkv_kernel, chunk=chunk, nseq=nseq),
        grid=(n // nseq, nc),
        in_specs=[seq_in(s) for _ in range(5) for s in range(nseq)] + [par] * 5 + [st],
        out_specs=[seq, st],
        out_shape=[jax.ShapeDtypeStruct((n, t, d), F32),
                   jax.ShapeDtypeStruct((n, RWKV_HEADS, RWKV_HEAD, RWKV_HEAD), F32)],
        scratch_shapes=[pltpu.VMEM((nseq * RWKV_HEADS, RWKV_HEAD, RWKV_HEAD), F32)],
        compiler_params=_cparams(("parallel", "arbitrary")),
        name="wkv",
    )(*r, *k, *v, *ld, *ag, vec(k_k), vec(k_a), vec(r_k), vec(gn_g), vec(gn_b), s0)


PAGES_PER_SEQ = 16
SEG_PER_PAGE = PAGE_SIZE // CMP_STRIDE
N_SEG = PAGES_PER_SEQ * SEG_PER_PAGE
N_CMP = N_SEG - 1
KV_COLS = KV_HEADS * HEAD_DIM


def _gelu_tanh(x):
    return x * (0.5 * (1.0 + jnp.tanh(np.sqrt(2.0 / np.pi) * (x + 0.044715 * (x * x * x)))))


def _compress_kernel(pt_ref, *refs):
    n_slab = 2 * KV_COLS // LANES
    pages = refs[:PAGES_PER_SEQ * n_slab]
    w1c_ref, w1_ref, pe_ref, b1_ref, w2_ref, b2_ref, kc_ref, vc_ref = refs[PAGES_PER_SEQ * n_slab:]
    valid = lax.broadcasted_iota(jnp.int32, (N_SEG, HEAD_DIM), 0) < N_CMP
    for typ, out_ref in ((0, kc_ref), (1, vc_ref)):
        pe8 = jnp.broadcast_to(pe_ref[typ], (8, CMP_BLOCK * HEAD_DIM))
        c0 = _bdot(pe8, w1_ref[typ])[0:1, :] + b1_ref[typ]
        for pair in range(KV_HEADS // 2):
            cb = typ * (KV_HEADS // 2) + pair
            acc = jnp.zeros((N_SEG, 4 * HEAD_DIM), F32)
            for p in range(CMP_STRIDE):
                xp = jnp.concatenate(
                    [pages[k * n_slab + cb][0, pl.ds(p, SEG_PER_PAGE, stride=CMP_STRIDE), :]
                     for k in range(PAGES_PER_SEQ)], axis=0)
                acc = acc + _bdot(xp, w1c_ref[typ, p])
            for gg in range(2):
                g = pair * 2 + gg
                first = acc[:, gg * LANES:gg * LANES + HEAD_DIM]
                second = acc[:, gg * LANES + HEAD_DIM:(gg + 1) * LANES]
                second_next = pltpu.roll(second, N_SEG - 1, 0)
                hid = _gelu_tanh(first + second_next + c0)
                out = _bdot(hid, w2_ref[typ]) + b2_ref[typ]
                out_ref[0, :, g * HEAD_DIM:(g + 1) * HEAD_DIM] = jnp.where(valid, out, 0.0)


def _compress(pages, layer, page_ids, n_seq, pe, w1, b1, w2, b2):
    w1c = w1.reshape(2, 2, CMP_STRIDE, HEAD_DIM, HEAD_DIM).transpose(0, 2, 3, 1, 4).reshape(
        2, CMP_STRIDE, HEAD_DIM, 2 * HEAD_DIM)
    zero = jnp.zeros_like(w1c)
    w1c = jnp.concatenate([jnp.concatenate([w1c, zero], axis=3), jnp.concatenate([zero, w1c], axis=3)],
                          axis=2)
    n_slab = 2 * KV_COLS // LANES
    slab0 = layer * (4 * KV_COLS // LANES)
    page_spec = lambda k, cb: pl.BlockSpec((1, PAGE_SIZE, LANES),
                                           lambda b, pt: (pt[b * PAGES_PER_SEQ + k], 0, slab0 + cb))
    full = lambda a: pl.BlockSpec(a.shape, lambda b, pt: (0,) * a.ndim)
    consts = (w1c, w1, pe.reshape(2, 1, CMP_BLOCK * HEAD_DIM), b1.reshape(2, 1, HEAD_DIM), w2,
              b2.reshape(2, 1, HEAD_DIM))
    out_spec = pl.BlockSpec((1, N_SEG, KV_COLS), lambda b, pt: (b, 0, 0))
    grid_spec = pltpu.PrefetchScalarGridSpec(
        num_scalar_prefetch=1,
        grid=(n_seq,),
        in_specs=([page_spec(k, cb) for k in range(PAGES_PER_SEQ) for cb in range(n_slab)]
                  + [full(a) for a in consts]),
        out_specs=[out_spec, out_spec],
    )
    return pl.pallas_call(
        _compress_kernel,
        grid_spec=grid_spec,
        out_shape=[jax.ShapeDtypeStruct((n_seq, N_SEG, KV_COLS), F32)] * 2,
        compiler_params=_cparams(("parallel",)),
        name="nsa_compress",
    )(page_ids, *([pages] * (PAGES_PER_SEQ * n_slab)), *consts)


def _overlap_matrix(n_blk):
    c0 = np.arange(N_CMP)[:, None] * CMP_STRIDE
    c1 = c0 + CMP_BLOCK - 1
    s0 = np.arange(n_blk)[None, :] * SEL_BLOCK
    s1 = s0 + SEL_BLOCK - 1
    ov = np.zeros((N_SEG, LANES), np.float32)
    ov[:N_CMP, :n_blk] = (c0 <= s1) & (c1 >= s0)
    return jnp.asarray(ov, BF16)


def _expand_matrix(n_keys):
    ex = (np.arange(LANES)[:, None] == (np.arange(n_keys)[None, :] // SEL_BLOCK)).astype(np.float32)
    return jnp.asarray(ex, BF16)


def _masked_softmax(s, mask):
    s = jnp.where(mask, s, NEG)
    m = jnp.max(s, -1, keepdims=True)
    e = jnp.where(mask, jnp.exp(s - m), 0.0)
    return e / jnp.maximum(jnp.sum(e, -1, keepdims=True), 1e-30)


def _select_blocks(imp, qblk, n_blk):
    lane = lax.broadcasted_iota(jnp.int32, imp.shape, 1)
    forced = (lane == 0) | (lane == qblk) | (lane == qblk - 1)
    imp = jnp.where(forced, jnp.inf, imp)
    imp = jnp.where(lane <= qblk, imp, -jnp.inf)
    rank = jnp.zeros(imp.shape, F32)
    for s in range(n_blk):
        c = imp[:, s:s + 1]
        tie_ahead = jnp.where(lane > s, 1.0, 0.0)
        rank = rank + jnp.where(c > imp, 1.0, jnp.where(c == imp, tie_ahead, 0.0))
    return jnp.where(rank < N_SEL, jnp.where(imp > -jnp.inf, 1.0, 0.0), 0.0)


def _select_blocks_t(imp, qblk, n_blk):
    blk = lax.broadcasted_iota(jnp.int32, imp.shape, 0)
    forced = (blk == 0) | (blk == qblk) | (blk == qblk - 1)
    imp = jnp.where(forced, jnp.inf, imp)
    imp = jnp.where(blk <= qblk, imp, -jnp.inf)
    rank = jnp.zeros(imp.shape, F32)
    for s in range(n_blk):
        c = imp[s:s + 1, :]
        tie_ahead = jnp.where(blk > s, 1.0, 0.0)
        rank = rank + jnp.where(c > imp, 1.0, jnp.where(c == imp, tie_ahead, 0.0))
    return jnp.where(rank < N_SEL, jnp.where(imp > -jnp.inf, 1.0, 0.0), 0.0)


def _softmax_rows_masked(s, mask):
    s = jnp.where(mask, s, NEG)
    m = jnp.max(s, 0, keepdims=True)
    e = jnp.where(mask, jnp.exp(s - m), 0.0)
    return e / jnp.maximum(jnp.sum(e, 0, keepdims=True), 1e-30)


def _nsa_prompt_kernel(q_ref, gt_ref, kc_ref, vc_ref, ks_ref, vs_ref, kw_ref, vw_ref, ovt_ref,
                       o_ref, qt_ref, mx_ref, m_ref, l_ref, acc_ref, oc_ref, *, n_chunks):
    qi = pl.program_id(2)
    scale = HEAD_DIM ** -0.5
    krow = lax.broadcasted_iota(jnp.int32, (QB, QB), 0)
    qcol = lax.broadcasted_iota(jnp.int32, (QB, QB), 1)
    qpos = qi * QB + qcol
    heads4 = lambda a: jnp.concatenate([a] * HPG, axis=1)

    m_ref[...] = jnp.full(m_ref.shape, NEG, F32)
    l_ref[...] = jnp.zeros(l_ref.shape, F32)
    acc_ref[...] = jnp.zeros(acc_ref.shape, F32)

    q_t = q_ref[...].T
    gt_t = jax.nn.sigmoid(gt_ref[...]).T
    cmp_seen = (krow < N_CMP) & (krow * CMP_STRIDE + (CMP_BLOCK - 1) <= qpos)
    mask_c = heads4(jnp.where(cmp_seen, 1.0, 0.0)) > 0.5
    qblk = (qi * QB + lax.broadcasted_iota(jnp.int32, (1, QB), 1)) // SEL_BLOCK
    n_blk = n_chunks * QB // SEL_BLOCK

    for g in range(2):
        for h in range(HPG):
            r0 = (g * HPG + h) * HEAD_DIM
            qt_ref[g, :, h * QB:(h + 1) * QB] = (q_t[r0:r0 + HEAD_DIM, :] * scale).astype(BF16)
        gs = slice(g * HEAD_DIM, (g + 1) * HEAD_DIM)
        p = _softmax_rows_masked(_bdot(kc_ref[0, :, gs], qt_ref[g]), mask_c)
        oc_ref[g] = _bdot_tn(vc_ref[0, :, gs], p)
        p_heads = p[:, 0:QB] + p[:, QB:2 * QB] + p[:, 2 * QB:3 * QB] + p[:, 3 * QB:4 * QB]
        imp = _dot_exact_lhs(ovt_ref[...], p_heads)
        sel = _select_blocks_t(imp[0:n_blk], qblk, n_blk)
        for c in range(n_chunks):
            lo, hi = 2 * c, 2 * c + 1
            mx_ref[g, c] = jnp.where(krow < SEL_BLOCK, sel[lo:lo + 1, :], sel[hi:hi + 1, :])

    def online_update(br, g, k, v, mask01):
        valid = heads4(mask01) > 0.5
        s = jnp.where(valid, _bdot(k, qt_ref[g]), NEG)
        m_old = m_ref[br, g]
        m_new = jnp.maximum(m_old, jnp.max(s, 0, keepdims=True))
        alpha = jnp.exp(m_old - m_new)
        p = jnp.where(valid, jnp.exp(s - m_new), 0.0)
        l_ref[br, g] = alpha * l_ref[br, g] + jnp.sum(p, 0, keepdims=True)
        acc_ref[br, g] = alpha * acc_ref[br, g] + _bdot_tn(v, p)
        m_ref[br, g] = m_new

    def sel_body(kc, carry):
        off = pl.multiple_of(kc * QB, QB)
        causal = (kc - qi) * QB + krow - qcol <= 0
        for g in range(2):
            k = ks_ref[pl.ds(off, QB), pl.ds(g * HEAD_DIM, HEAD_DIM)]
            v = vs_ref[pl.ds(off, QB), pl.ds(g * HEAD_DIM, HEAD_DIM)]
            online_update(0, g, k, v, jnp.where(causal, mx_ref[g, kc], 0.0))
        return carry

    lax.fori_loop(0, qi + 1, sel_body, 0)

    def win_body(kc, carry):
        off = pl.multiple_of(kc * QB, QB)
        dist = (qi - kc) * QB + qcol - krow
        band = jnp.where((dist >= 0) & (dist < WINDOW), 1.0, 0.0)
        for g in range(2):
            k = kw_ref[pl.ds(off, QB), pl.ds(g * HEAD_DIM, HEAD_DIM)]
            v = vw_ref[pl.ds(off, QB), pl.ds(g * HEAD_DIM, HEAD_DIM)]
            online_update(1, g, k, v, band)
        return carry

    lax.fori_loop(jnp.maximum(qi - WINDOW // QB, 0), qi + 1, win_body, 0)

    pieces = []
    for g in range(2):
        o_sel = acc_ref[0, g] / jnp.maximum(l_ref[0, g], 1e-30)
        o_win = acc_ref[1, g] / jnp.maximum(l_ref[1, g], 1e-30)
        o_cmp = oc_ref[g]
        for h in range(HPG):
            cols = slice(h * QB, (h + 1) * QB)
            gc = (g * HPG + h) * 3
            pieces.append(gt_t[gc:gc + 1, :] * o_cmp[:, cols] + gt_t[gc + 1:gc + 2, :] * o_sel[:, cols]
                          + gt_t[gc + 2:gc + 3, :] * o_win[:, cols])
    o_ref[...] = jnp.concatenate(pieces, axis=0).T


def _nsa_prompt(q, gates, kc, vc, rows, win, b, t):
    nq = t // QB
    pair = 2 * HEAD_DIM
    blk = lambda shape, fn: pl.BlockSpec(shape, fn)
    seq_cols = lambda cb: blk((t, pair), lambda i, gp, qi: (i, cb + gp))
    return pl.pallas_call(
        functools.partial(_nsa_prompt_kernel, n_chunks=nq),
        grid=(b, KV_HEADS // 2, nq),
        in_specs=[blk((QB, 2 * HPG * HEAD_DIM), lambda i, gp, qi: (i * nq + qi, gp)),
                  blk((QB, LANES), lambda i, gp, qi: (i * nq + qi, gp)),
                  blk((1, N_SEG, pair), lambda i, gp, qi: (i, 0, gp)),
                  blk((1, N_SEG, pair), lambda i, gp, qi: (i, 0, gp)),
                  seq_cols(4), seq_cols(6),
                  seq_cols(0), seq_cols(2),
                  blk((LANES, N_SEG), lambda i, gp, qi: (0, 0))],
        out_specs=blk((QB, 2 * HPG * HEAD_DIM), lambda i, gp, qi: (i * nq + qi, gp)),
        out_shape=jax.ShapeDtypeStruct((b * t, N_HEADS * HEAD_DIM), F32),
        scratch_shapes=[pltpu.VMEM((2, HEAD_DIM, HPG * QB), BF16),
                        pltpu.VMEM((2, nq, QB, QB), F32),
                        pltpu.VMEM((2, 2, 1, HPG * QB), F32),
                        pltpu.VMEM((2, 2, 1, HPG * QB), F32),
                        pltpu.VMEM((2, 2, HEAD_DIM, HPG * QB), F32),
                        pltpu.VMEM((2, HEAD_DIM, HPG * QB), F32)],
        compiler_params=_cparams(("parallel", "parallel", "arbitrary")),
        name="nsa_prompt",
    )(q, gates, kc, vc, rows, rows, win, win, _overlap_matrix(t // SEL_BLOCK).T)


TQ = 8


def _nsa_sample_kernel(pt_ref, *refs, past_len, n_new):
    pages = refs[:PAGES_PER_SEQ]
    q_ref, gt_ref, kc_ref, vc_ref, new_ref, wb_ref, wn_ref, ov_ref, ex_ref, o_ref = refs[PAGES_PER_SEQ:]
    scale = HEAD_DIM ** -0.5
    rows = HPG * TQ
    tok1 = lax.broadcasted_iota(jnp.int32, (TQ, 1), 0)
    qpos1 = past_len + tok1
    lane = lax.broadcasted_iota(jnp.int32, (TQ, LANES), 1)
    mask_c = (lane < N_CMP) & (lane * CMP_STRIDE + (CMP_BLOCK - 1) <= qpos1)
    newj = lax.broadcasted_iota(jnp.int32, (TQ, TQ), 1)
    newt = lax.broadcasted_iota(jnp.int32, (TQ, TQ), 0)
    mask_new = (newj < n_new) & (newj <= newt)
    wcache = wb_ref.shape[1]
    wi = lax.broadcasted_iota(jnp.int32, (TQ, wcache), 1)
    wpos = past_len - wcache + wi
    mask_w = (wpos <= qpos1) & (wpos > qpos1 - WINDOW)
    n_blk = -(-(past_len + n_new) // SEL_BLOCK)
    gt = jax.nn.sigmoid(gt_ref[0])

    def bcast(mask):
        return jnp.broadcast_to(mask[None], (HPG,) + mask.shape)

    def two_part_attention(s_a, mask_a, s_b, mask_b):
        s_a = jnp.where(mask_a, s_a, NEG)
        s_b = jnp.where(mask_b, s_b, NEG)
        m = jnp.maximum(jnp.max(s_a, -1, keepdims=True), jnp.max(s_b, -1, keepdims=True))
        e_a = jnp.where(mask_a, jnp.exp(s_a - m), 0.0)
        e_b = jnp.where(mask_b, jnp.exp(s_b - m), 0.0)
        den = jnp.maximum(jnp.sum(e_a, -1, keepdims=True) + jnp.sum(e_b, -1, keepdims=True), 1e-30)
        return e_a / den, e_b / den

    for g in range(KV_HEADS):
        gs = slice(g * HEAD_DIM, (g + 1) * HEAD_DIM)
        vs_cols = slice(KV_COLS + g * HEAD_DIM, KV_COLS + (g + 1) * HEAD_DIM)
        qg = jnp.concatenate(
            [q_ref[0, :, (g * HPG + h) * HEAD_DIM:(g * HPG + h + 1) * HEAD_DIM] for h in range(HPG)],
            axis=0) * scale
        qg = qg.astype(BF16)
        s = _bdot_nt(qg, kc_ref[0, :, gs]).reshape(HPG, TQ, N_SEG)
        p = _masked_softmax(s, mask_c[None])
        o_cmp = _bdot(p.reshape(rows, N_SEG), vc_ref[0, :, gs])
        imp = _dot_exact_rhs(p[0] + p[1] + p[2] + p[3], ov_ref[...])
        sel = _select_blocks(imp, qpos1 // SEL_BLOCK, n_blk)
        past_mask = jnp.dot(sel.astype(BF16), ex_ref[...], preferred_element_type=F32) > 0.5
        new_blk = past_len // SEL_BLOCK
        mask_sn = mask_new & (sel[:, new_blk:new_blk + 1] > 0.5)
        s_past = jnp.concatenate([_bdot_nt(qg, pg[0, :, gs]) for pg in pages], axis=1)
        s_new = _bdot_nt(qg, new_ref[0, :, 2 * KV_COLS + g * HEAD_DIM:2 * KV_COLS + (g + 1) * HEAD_DIM])
        p_past, p_new = two_part_attention(s_past.reshape(HPG, TQ, -1), bcast(past_mask),
                                           s_new.reshape(HPG, TQ, TQ), bcast(mask_sn))
        p_past = p_past.reshape(rows, -1)
        o_sel = _bdot(p_new.reshape(rows, TQ),
                      new_ref[0, :, 3 * KV_COLS + g * HEAD_DIM:3 * KV_COLS + (g + 1) * HEAD_DIM])
        for k, pg in enumerate(pages):
            o_sel = o_sel + _bdot(p_past[:, k * PAGE_SIZE:(k + 1) * PAGE_SIZE], pg[0, :, vs_cols])
        s_wb = _bdot_nt(qg, wb_ref[0, :, gs])
        s_wn = _bdot_nt(qg, wn_ref[0, :, gs])
        p_wb, p_wn = two_part_attention(s_wb.reshape(HPG, TQ, wcache), bcast(mask_w),
                                        s_wn.reshape(HPG, TQ, TQ), bcast(mask_new))
        o_win = (_bdot(p_wb.reshape(rows, wcache), wb_ref[0, :, vs_cols])
                 + _bdot(p_wn.reshape(rows, TQ), wn_ref[0, :, vs_cols]))
        for h in range(HPG):
            r8 = slice(h * TQ, (h + 1) * TQ)
            gc = (g // 2) * LANES + ((g % 2) * HPG + h) * 3
            o = (gt[:, gc:gc + 1] * o_cmp[r8] + gt[:, gc + 1:gc + 2] * o_sel[r8]
                 + gt[:, gc + 2:gc + 3] * o_win[r8])
            c0 = (g * HPG + h) * HEAD_DIM
            o_ref[0, :, c0:c0 + HEAD_DIM] = o


def _nsa_sample(pages, layer, page_ids, q, gates, kc, vc, new_rows, win_buf, win_new, past_len, n_new):
    n = q.shape[0]
    wc = win_buf.shape[1]
    n_blk = -(-(past_len + n_new) // SEL_BLOCK)
    page_spec = lambda k: pl.BlockSpec((1, PAGE_SIZE, 2 * KV_COLS),
                                       lambda b, pt: (pt[b * PAGES_PER_SEQ + k], 0, 2 * layer + 1))
    seq = lambda a: pl.BlockSpec((1,) + a.shape[1:], lambda b, pt: (b,) + (0,) * (a.ndim - 1))
    full = lambda a: pl.BlockSpec(a.shape, lambda b, pt: (0,) * a.ndim)
    ov, ex = _overlap_matrix(n_blk), _expand_matrix(past_len)
    per_seq = (q, gates, kc, vc, new_rows, win_buf, win_new)
    grid_spec = pltpu.PrefetchScalarGridSpec(
        num_scalar_prefetch=1,
        grid=(n,),
        in_specs=([page_spec(k) for k in range(PAGES_PER_SEQ)] + [seq(a) for a in per_seq]
                  + [full(ov), full(ex)]),
        out_specs=pl.BlockSpec((1, TQ, N_HEADS * HEAD_DIM), lambda b, pt: (b, 0, 0)),
    )
    return pl.pallas_call(
        functools.partial(_nsa_sample_kernel, past_len=past_len, n_new=n_new),
        grid_spec=grid_spec,
        out_shape=jax.ShapeDtypeStruct((n, TQ, N_HEADS * HEAD_DIM), F32),
        compiler_params=_cparams(("parallel",)),
        name="nsa_sample",
    )(page_ids, *([pages] * PAGES_PER_SEQ), *per_seq, ov, ex)


WKV_SAMPLE_CHUNK = 16
WKV_PROMPT_SEQS = 2
WKV_SAMPLE_SEQS = 8


def _pad_rows(a, n):
    return jnp.pad(a, ((0, 0), (0, n - a.shape[1]), (0, 0)))


def _nsa_layer(x, n_prompt, b, t, db, ds, pages, layer, page_table, win_cache, w_in, w_out, cmp, ln_g,
               ln_b):
    d = x.shape[1]
    qd = N_HEADS * HEAD_DIM
    q = _proj(x, w_in, 512, 0, qd)
    rows = _proj(x, w_in, 512, 2, 4 * KV_COLS)
    win = _proj(x, w_in, 512, 4, 2 * KV_COLS)
    n_gate = 2 * HPG * 3
    g0 = qd + 6 * KV_COLS
    wg = jnp.concatenate(
        [jnp.pad(w_in[:, g0 + p * n_gate:g0 + (p + 1) * n_gate], ((0, 0), (0, LANES - n_gate)))
         for p in range(KV_HEADS // 2)], axis=1)
    gates = _proj(x, wg, wg.shape[1], 0, wg.shape[1])
    past_len = page_table.shape[1] * PAGE_SIZE

    prompt_pages = rows.reshape(rows.shape[0] // PAGE_SIZE, PAGE_SIZE, 4 * KV_COLS)
    kc_p, vc_p = _compress(prompt_pages, 0, jnp.arange(b * PAGES_PER_SEQ, dtype=jnp.int32), b, *cmp)
    o_p = _nsa_prompt(q, gates, kc_p, vc_p, rows, win, b, t)

    q_s, rows_s, win_s, gates_s = (a[n_prompt:].reshape(db, ds, -1) for a in (q, rows, win, gates))
    page_ids = page_table.reshape(-1).astype(jnp.int32)
    kc_s, vc_s = _compress(pages, layer, page_ids, db, *cmp)
    win_buf = win_cache.reshape(db, win_cache.shape[1], 2 * KV_COLS)
    o_s = _nsa_sample(pages, layer, page_ids, _pad_rows(q_s, TQ), _pad_rows(gates_s, TQ), kc_s, vc_s,
                      _pad_rows(rows_s, TQ), win_buf, _pad_rows(win_s, TQ), past_len, ds)[:, :ds]

    o = jnp.concatenate([o_p, o_s.reshape(db * ds, qd)], axis=0)
    x_new = _out_ln(o, None, w_out, x, ln_g, ln_b)
    keep = min(WINDOW, t)
    win_p = win[:n_prompt].reshape(b, t, 2 * KV_COLS)
    outs = (rows[:n_prompt].reshape(b, t, 4, KV_HEADS, HEAD_DIM),
            win_p[:, t - keep:].reshape(b, keep, 2, KV_HEADS, HEAD_DIM),
            rows_s.reshape(db, ds, 4, KV_HEADS, HEAD_DIM),
            jnp.concatenate([win_buf, win_s], axis=1)[:, ds:].reshape(db, -1, 2, KV_HEADS, HEAD_DIM))
    return x_new, outs


def _rwkv_layer(x, n_prompt, b, t, db, ds, shift_s, state_s, mu, w_rkv, w0, w1, w2, a0, a1, a2, g1, g2,
                k_k, k_a, r_k, gn_g, gn_b, w_out, ln_g, ln_b):
    d = x.shape[1]
    xs = x[n_prompt:].reshape(db, ds, d)
    prev_s = jnp.concatenate([shift_s[:, None].astype(x.dtype), xs[:, :-1]], axis=1).reshape(db * ds, d)
    tiles, p_tiles = x.shape[0] // ROW_TILE, n_prompt // ROW_TILE
    last_rows = x[ROW_TILE - 1:n_prompt:ROW_TILE]
    carry = jnp.concatenate([jnp.zeros((1, d), x.dtype), last_rows[:-1]], axis=0)
    seq_start = (jnp.arange(p_tiles) * ROW_TILE) % t == 0
    carry = jnp.where(seq_start[:, None], 0.0, carry)
    carry = jnp.pad(carry, ((0, tiles - p_tiles), (0, 0))).reshape(tiles, 1, d)
    r, k, v, ld, ag, g = _rwkv_pre(x, carry, prev_s, mu, w_rkv, w0, w1, w2, a0, a1, a2, g1, g2)
    streams = (r, k, v, ld, ag)
    head_params = (k_k, k_a, r_k.reshape(d), gn_g, gn_b)
    zero_state = jnp.zeros((b, RWKV_HEADS, RWKV_HEAD, RWKV_HEAD), F32)
    y_p, wkv_p = _wkv(streams, b, t, zero_state, *head_params, WKV_CHUNK, WKV_PROMPT_SEQS)
    streams_s = [_pad_rows(a[n_prompt:].reshape(db, ds, d), WKV_SAMPLE_CHUNK).reshape(-1, d)
                 for a in streams]
    y_s, wkv_s = _wkv(streams_s, db, WKV_SAMPLE_CHUNK, state_s.astype(F32), *head_params,
                      WKV_SAMPLE_CHUNK, WKV_SAMPLE_SEQS)
    y = jnp.concatenate([y_p.reshape(n_prompt, d), y_s[:, :ds].reshape(db * ds, d)], axis=0)
    x_new = _out_ln(y, g, w_out, x, ln_g, ln_b)
    return x_new, (wkv_p, x[t - 1:n_prompt:t], wkv_s.astype(state_s.dtype), xs[:, -1])


def kernel(x_prompt, x_sample, cache_nsa_paged, cache_nsa_win, state_rwkv_wkv, state_rwkv_shift, page_table, nsa_w_in, nsa_w_out, nsa_cmp_pe, nsa_cmp_w1, nsa_cmp_b1, nsa_cmp_w2, nsa_cmp_b2, rwkv_mu, rwkv_w_rkv, rwkv_w0, rwkv_w1, rwkv_w2, rwkv_a0, rwkv_a1, rwkv_a2, rwkv_g1, rwkv_g2, rwkv_k_k, rwkv_k_a, rwkv_r_k, rwkv_gn_g, rwkv_gn_b, rwkv_w_out, ln_g, ln_b, moe_w_router, moe_b_router, moe_w_up, moe_b_up, moe_w_down, moe_b_down):
    b, t, d = x_prompt.shape
    db, ds, _ = x_sample.shape
    n_prompt = b * t
    x = jnp.concatenate([x_prompt.reshape(n_prompt, d), x_sample.reshape(db * ds, d)], axis=0)
    n_phys = cache_nsa_paged.shape[0]
    nsa_outs, rwkv_outs = [], []
    for i in range(DEPTH):
        j = i // 2
        if i % 2 == 0:
            pages = cache_nsa_paged.reshape(n_phys, PAGE_SIZE, -1)
            cmp = (nsa_cmp_pe[j], nsa_cmp_w1[j], nsa_cmp_b1[j], nsa_cmp_w2[j], nsa_cmp_b2[j])
            x, outs = _nsa_layer(x, n_prompt, b, t, db, ds, pages, j, page_table, cache_nsa_win[j],
                                 nsa_w_in[j], nsa_w_out[j], cmp, ln_g[i, 0], ln_b[i, 0])
            nsa_outs.append(outs)
        else:
            x, outs = _rwkv_layer(x, n_prompt, b, t, db, ds, state_rwkv_shift[j], state_rwkv_wkv[j],
                                  rwkv_mu[j], rwkv_w_rkv[j], rwkv_w0[j], rwkv_w1[j], rwkv_w2[j],
                                  rwkv_a0[j], rwkv_a1[j], rwkv_a2[j], rwkv_g1[j], rwkv_g2[j],
                                  rwkv_k_k[j], rwkv_k_a[j], rwkv_r_k[j], rwkv_gn_g[j], rwkv_gn_b[j],
                                  rwkv_w_out[j], ln_g[i, 0], ln_b[i, 0])
            rwkv_outs.append(outs)
        x = _moe_layer(x, moe_w_router[i], moe_b_router[i], moe_w_up[i], moe_b_up[i], moe_w_down[i],
                       moe_b_down[i], ln_g[i, 1], ln_b[i, 1])
    p_rows, p_win, s_rows, s_win = (jnp.stack(z, axis=k) for z, k in zip(zip(*nsa_outs), (2, 0, 2, 0)))
    p_wkv, p_shift, s_wkv, s_shift = (jnp.stack(z, axis=0) for z in zip(*rwkv_outs))
    return (x[:n_prompt].reshape(b, t, d), x[n_prompt:].reshape(db, ds, d),
            p_rows, p_win, p_wkv, p_shift, s_rows, s_win, s_wkv, s_shift)
```

```python
import functools

import numpy as np
import jax
import jax.numpy as jnp
from jax import lax
from jax.experimental import pallas as pl
from jax.experimental.pallas import tpu as pltpu

F32 = jnp.float32
BF16 = jnp.bfloat16

D_MODEL = 1024
DEPTH = 2
N_HEADS = 16
HEAD_DIM = 64
KV_HEADS = 4
HPG = N_HEADS // KV_HEADS
CMP_BLOCK = 32
CMP_STRIDE = 16
SEL_BLOCK = 64
N_SEL = 16
WINDOW = 512
PAGE_SIZE = 128
RWKV_HEAD = 64
RWKV_HEADS = D_MODEL // RWKV_HEAD
GN_EPS = 64e-5
N_EXPERTS = 32
TOP_K = 4
D_EXPERT = D_MODEL
SWIGLU_ALPHA = 1.702
SWIGLU_LIMIT = 7.0
LN_EPS = 1e-5
DN_ALPHA = (2 * DEPTH) ** 0.25

LANES = 128
QB = 128
KEY_STEP = 256
NEG = -1e30
VMEM_LIMIT = 56 * 1024 * 1024
ROW_TILE = 512
MOE_ROWS = 256
WKV_CHUNK = 64


def _cparams(sem):
    return pltpu.CompilerParams(dimension_semantics=sem, vmem_limit_bytes=VMEM_LIMIT)


def _bdot(a, b):
    return jnp.dot(a.astype(BF16), b.astype(BF16), preferred_element_type=F32)


def _bdot_nt(a, b):
    return lax.dot_general(a.astype(BF16), b.astype(BF16), (((1,), (1,)), ((), ())),
                           preferred_element_type=F32)


def _bdot_tn(a, b):
    return lax.dot_general(a.astype(BF16), b.astype(BF16), (((0,), (0,)), ((), ())),
                           preferred_element_type=F32)


def _bmm(a, b):
    return lax.dot_general(a.astype(BF16), b.astype(BF16), (((2,), (1,)), ((0,), (0,))),
                           preferred_element_type=F32)


def _bmm_nt(a, b):
    return lax.dot_general(a.astype(BF16), b.astype(BF16), (((2,), (2,)), ((0,), (0,))),
                           preferred_element_type=F32)


def _split3(x):
    hi = x.astype(BF16)
    r = x - hi.astype(F32)
    mid = r.astype(BF16)
    lo = (r - mid.astype(F32)).astype(BF16)
    return hi, mid, lo


def _dot_exact_rhs(x, w01):
    hi, mid, lo = _split3(x)
    d = lambda a: jnp.dot(a, w01, preferred_element_type=F32)
    return d(hi) + d(mid) + d(lo)


def _dot_exact_lhs(w01, x):
    hi, mid, lo = _split3(x)
    d = lambda a: jnp.dot(w01, a, preferred_element_type=F32)
    return d(hi) + d(mid) + d(lo)


def _layer_norm(z, g, b):
    mu = jnp.mean(z, -1, keepdims=True)
    zc = z - mu
    var = jnp.mean(zc * zc, -1, keepdims=True)
    return zc * lax.rsqrt(var + LN_EPS) * g + b


def _proj_kernel(x_ref, w_ref, o_ref):
    o_ref[...] = _bdot(x_ref[...], w_ref[...])


def _proj(x, w, tn, col_block0, n_out):
    m, k = x.shape
    return pl.pallas_call(
        _proj_kernel,
        grid=(n_out // tn, m // ROW_TILE),
        in_specs=[pl.BlockSpec((ROW_TILE, k), lambda j, i: (i, 0)),
                  pl.BlockSpec((k, tn), lambda j, i: (0, j + col_block0))],
        out_specs=pl.BlockSpec((ROW_TILE, tn), lambda j, i: (i, j)),
        out_shape=jax.ShapeDtypeStruct((m, n_out), F32),
        compiler_params=_cparams(("parallel", "parallel")),
        name="proj",
    )(x, w)


def _out_ln_kernel(*refs, gated):
    if gated:
        a_ref, m_ref, w_ref, res_ref, g_ref, b_ref, o_ref = refs
        a = a_ref[...] * m_ref[...]
    else:
        a_ref, w_ref, res_ref, g_ref, b_ref, o_ref = refs
        a = a_ref[...]
    z = DN_ALPHA * res_ref[...] + _bdot(a, w_ref[...])
    o_ref[...] = _layer_norm(z, g_ref[...], b_ref[...])


def _out_ln(a, mul, w, res, g, b):
    m, k = a.shape
    n = w.shape[1]
    row = lambda i: (i, 0)
    fixed = lambda i: (0, 0)
    ins = [a] + ([mul] if mul is not None else []) + [w, res, g.reshape(1, n), b.reshape(1, n)]
    specs = ([pl.BlockSpec((ROW_TILE, k), row)] * (2 if mul is not None else 1)
             + [pl.BlockSpec((k, n), fixed), pl.BlockSpec((ROW_TILE, n), row),
                pl.BlockSpec((1, n), fixed), pl.BlockSpec((1, n), fixed)])
    return pl.pallas_call(
        functools.partial(_out_ln_kernel, gated=mul is not None),
        grid=(m // ROW_TILE,),
        in_specs=specs,
        out_specs=pl.BlockSpec((ROW_TILE, n), row),
        out_shape=jax.ShapeDtypeStruct((m, n), F32),
        compiler_params=_cparams(("parallel",)),
        name="out_ln",
    )(*ins)


def _router_kernel(x_ref, w_ref, b_ref, e_ref, g_ref):
    xh, xm, xl = _split3(x_ref[...])
    wh, wm, wl = _split3(w_ref[...])
    d = lambda a, c: jnp.dot(a, c, preferred_element_type=F32)
    acc = d(xh, wh) + (d(xh, wm) + d(xm, wh)) + (d(xm, wm) + d(xh, wl) + d(xl, wh))
    lane = lax.broadcasted_iota(jnp.int32, acc.shape, 1).astype(F32)
    logits = jnp.where(lane < N_EXPERTS, acc + b_ref[...], -jnp.inf)
    vals, ids = [], []
    for _ in range(TOP_K):
        m = jnp.max(logits, -1, keepdims=True)
        idx = jnp.min(jnp.where(logits == m, lane, float(LANES)), -1, keepdims=True)
        vals.append(m)
        ids.append(idx)
        logits = jnp.where(lane == idx, -jnp.inf, logits)
    es = [jnp.exp(v - vals[0]) for v in vals]
    den = functools.reduce(lambda a, c: a + c, es)
    e_out = jnp.zeros(acc.shape, F32)
    g_out = jnp.zeros(acc.shape, F32)
    for k in range(TOP_K):
        e_out = jnp.where(lane == k, ids[k], e_out)
        g_out = jnp.where(lane == k, es[k] / den, g_out)
    e_ref[...] = e_out
    g_ref[...] = g_out


def _router(x, w_pad, b_pad):
    m, k = x.shape
    out = pl.BlockSpec((ROW_TILE, LANES), lambda i: (i, 0))
    return pl.pallas_call(
        _router_kernel,
        grid=(m // ROW_TILE,),
        in_specs=[pl.BlockSpec((ROW_TILE, k), lambda i: (i, 0)),
                  pl.BlockSpec((k, LANES), lambda i: (0, 0)),
                  pl.BlockSpec((1, LANES), lambda i: (0, 0))],
        out_specs=[out, out],
        out_shape=[jax.ShapeDtypeStruct((m, LANES), F32)] * 2,
        compiler_params=_cparams(("parallel",)),
        name="router",
    )(x, w_pad, b_pad)


def _moe_kernel(be_ref, nb_ref, x_ref, wu_ref, bu_ref, wd_ref, bd_ref, gw_ref, o_ref):
    i = pl.program_id(0)

    @pl.when(i < nb_ref[0])
    def _():
        u = _bdot(x_ref[...], wu_ref[0]) + bu_ref[0]
        glu = jnp.minimum(u[:, :D_EXPERT], SWIGLU_LIMIT)
        lin = jnp.clip(u[:, D_EXPERT:], -SWIGLU_LIMIT, SWIGLU_LIMIT)
        h = glu * jax.nn.sigmoid(SWIGLU_ALPHA * glu) * (lin + 1.0)
        y = _bdot(h, wd_ref[0]) + bd_ref[0]
        o_ref[...] = y * gw_ref[...]

    @pl.when(i >= nb_ref[0])
    def _():
        o_ref[...] = jnp.zeros_like(o_ref)


def _moe_experts(xg, gw, block_e, n_used, w_up, b_up, w_down, b_down):
    r, d = xg.shape
    nblk = r // MOE_ROWS
    e, _, d2 = w_up.shape
    grid_spec = pltpu.PrefetchScalarGridSpec(
        num_scalar_prefetch=2,
        grid=(nblk,),
        in_specs=[pl.BlockSpec((MOE_ROWS, d), lambda i, be, nb: (i, 0)),
                  pl.BlockSpec((1, d, d2), lambda i, be, nb: (be[i], 0, 0)),
                  pl.BlockSpec((1, 1, d2), lambda i, be, nb: (be[i], 0, 0)),
                  pl.BlockSpec((1, d2 // 2, d), lambda i, be, nb: (be[i], 0, 0)),
                  pl.BlockSpec((1, 1, d), lambda i, be, nb: (be[i], 0, 0)),
                  pl.BlockSpec((MOE_ROWS, 1), lambda i, be, nb: (i, 0))],
        out_specs=pl.BlockSpec((MOE_ROWS, d), lambda i, be, nb: (i, 0)),
    )
    return pl.pallas_call(
        _moe_kernel,
        grid_spec=grid_spec,
        out_shape=jax.ShapeDtypeStruct((r, d), F32),
        compiler_params=_cparams(("arbitrary",)),
        name="moe_experts",
    )(block_e, n_used, xg, w_up, b_up.reshape(e, 1, d2), w_down, b_down.reshape(e, 1, d), gw)


def _combine_ln_kernel(res_ref, y_ref, g_ref, b_ref, o_ref):
    y = y_ref[0] + y_ref[1] + y_ref[2] + y_ref[3]
    o_ref[...] = _layer_norm(DN_ALPHA * res_ref[...] + y, g_ref[...], b_ref[...])


def _combine_ln(res, y4, g, b):
    m, n = res.shape
    return pl.pallas_call(
        _combine_ln_kernel,
        grid=(m // ROW_TILE,),
        in_specs=[pl.BlockSpec((ROW_TILE, n), lambda i: (i, 0)),
                  pl.BlockSpec((TOP_K, ROW_TILE, n), lambda i: (0, i, 0)),
                  pl.BlockSpec((1, n), lambda i: (0, 0)),
                  pl.BlockSpec((1, n), lambda i: (0, 0))],
        out_specs=pl.BlockSpec((ROW_TILE, n), lambda i: (i, 0)),
        out_shape=jax.ShapeDtypeStruct((m, n), F32),
        compiler_params=_cparams(("parallel",)),
        name="combine_ln",
    )(res, y4, g.reshape(1, n), b.reshape(1, n))


def _moe_layer(x, w_router, b_router, w_up, b_up, w_down, b_down, ln_g, ln_b):
    n = x.shape[0]
    wr = jnp.pad(w_router, ((0, 0), (0, LANES - N_EXPERTS)))
    br = jnp.pad(b_router, (0, LANES - N_EXPERTS)).reshape(1, LANES)
    ids, gates = _router(x, wr, br)
    top_e = ids[:, :TOP_K].astype(jnp.int32)
    gate = gates[:, :TOP_K]
    n_assign = n * TOP_K
    flat_e = top_e.reshape(-1).astype(jnp.int32)
    iota = jnp.arange(n_assign, dtype=jnp.int32)
    e_sorted, order = lax.sort((flat_e, iota), num_keys=1)
    _, sorted_pos = lax.sort((order, iota), num_keys=1)
    experts = jnp.arange(N_EXPERTS, dtype=jnp.int32)
    counts = jnp.sum((flat_e[None, :] == experts[:, None]).astype(jnp.int32), axis=1)
    raw_start = jnp.cumsum(counts) - counts
    padded = (counts + MOE_ROWS - 1) // MOE_ROWS * MOE_ROWS
    pad_end = jnp.cumsum(padded)
    pad_start = pad_end - padded
    n_blocks = -(-n_assign // MOE_ROWS) + N_EXPERTS
    n_rows = n_blocks * MOE_ROWS
    block_start = jnp.arange(n_blocks, dtype=jnp.int32) * MOE_ROWS
    block_e = jnp.minimum(jnp.sum(pad_end[None, :] <= block_start[:, None], axis=1),
                          N_EXPERTS - 1).astype(jnp.int32)
    n_used = (pad_end[-1:] // MOE_ROWS).astype(jnp.int32)
    j = (block_start - pad_start[block_e])[:, None] + jnp.arange(MOE_ROWS, dtype=jnp.int32)[None, :]
    valid = (j < counts[block_e][:, None]).reshape(-1)
    src = jnp.clip(raw_start[block_e][:, None] + j, 0, n_assign - 1).reshape(-1)
    assign_of_row = order[src]
    tok_of_row = jnp.where(valid, assign_of_row // TOP_K, 0)
    gate_of_row = jnp.where(valid, gate.reshape(-1)[assign_of_row], 0.0)
    row_of_assign = (pad_start - raw_start)[flat_e] + sorted_pos
    xg = x[tok_of_row]
    yb = _moe_experts(xg, gate_of_row.reshape(n_rows, 1), block_e, n_used, w_up, b_up, w_down, b_down)
    y4 = yb[row_of_assign.reshape(n, TOP_K).T]
    return _combine_ln(x, y4, ln_g, ln_b)


def _rwkv_pre_kernel(x_ref, carry_ref, xps_ref, mu_ref, wrkv_ref, w0_ref, w1_ref, w2_ref, a0_ref, a1_ref,
                     a2_ref, g1_ref, g2_ref, r_ref, k_ref, v_ref, ld_ref, ag_ref, g_ref, *, prompt_tiles):
    x = x_ref[...]
    first = lax.broadcasted_iota(jnp.int32, x.shape, 0) == 0
    x_prev = jnp.where(first, carry_ref[0], pltpu.roll(x, 1, 0))
    x_prev = jnp.where(pl.program_id(0) >= prompt_tiles, xps_ref[...], x_prev)
    dx = x_prev - x
    mix = lambda s: x + dx * mu_ref[s:s + 1, :]
    r_ref[...] = _bdot(mix(0), wrkv_ref[0])
    k_ref[...] = _bdot(mix(1), wrkv_ref[1])
    v_ref[...] = _bdot(mix(2), wrkv_ref[2])
    lw = w0_ref[...] + _bdot(jnp.tanh(_bdot(mix(3), w1_ref[...])), w2_ref[...])
    z = -lw
    softplus = jnp.maximum(z, 0.0) + jnp.log1p(jnp.exp(-jnp.abs(z)))
    ld_ref[...] = -jnp.exp(-softplus - 0.5)
    ag_ref[...] = jax.nn.sigmoid(a0_ref[...] + _bdot(_bdot(mix(4), a1_ref[...]), a2_ref[...]))
    g_ref[...] = _bdot(jax.nn.sigmoid(_bdot(mix(5), g1_ref[...])), g2_ref[...])


def _rwkv_pre(x, carry, x_prev_sample, mu, w_rkv, w0, w1, w2, a0, a1, a2, g1, g2):
    m, d = x.shape
    prompt_tiles = (m - x_prev_sample.shape[0]) // ROW_TILE
    row = pl.BlockSpec((ROW_TILE, d), lambda i: (i, 0))
    full = lambda a: pl.BlockSpec(a.shape, lambda i: (0,) * a.ndim)
    w0, a0 = w0.reshape(1, d), a0.reshape(1, d)
    weights = (mu, w_rkv, w0, w1, w2, a0, a1, a2, g1, g2)
    return pl.pallas_call(
        functools.partial(_rwkv_pre_kernel, prompt_tiles=prompt_tiles),
        grid=(m // ROW_TILE,),
        in_specs=[row, pl.BlockSpec((1, 1, d), lambda i: (i, 0, 0)),
                  pl.BlockSpec((ROW_TILE, d), lambda i: (jnp.maximum(i - prompt_tiles, 0), 0))]
                 + [full(a) for a in weights],
        out_specs=[row] * 6,
        out_shape=[jax.ShapeDtypeStruct((m, d), F32)] * 6,
        compiler_params=_cparams(("parallel",)),
        name="rwkv_pre",
    )(x, carry, x_prev_sample, *weights)


def _wkv_kernel(*refs, chunk, nseq):
    streams = [refs[j * nseq:(j + 1) * nseq] for j in range(5)]
    kk_ref, ka_ref, rk_ref, gg_ref, gb_ref, s0_ref, y_ref, sout_ref, s_ref = refs[5 * nseq:]
    c = pl.program_id(1)
    nh = nseq * RWKV_HEADS

    @pl.when(c == 0)
    def _():
        s_ref[...] = s0_ref[...].reshape(nh, RWKV_HEAD, RWKV_HEAD)

    row = lax.broadcasted_iota(jnp.int32, (chunk, chunk), 0)
    col = lax.broadcasted_iota(jnp.int32, (chunk, chunk), 1)
    incl = (col <= row)[None]
    strict = (col < row)[None]
    tri = jnp.where(col <= row, 1.0, 0.0).astype(BF16)
    eye = jnp.where(col == row, 1.0, 0.0).astype(F32)[None]

    def heads(rows_of_seq):
        return jnp.stack([rows_of_seq(i)[:, h * RWKV_HEAD:(h + 1) * RWKV_HEAD]
                          for i in range(nseq) for h in range(RWKV_HEADS)], axis=0)

    def head_param(ref):
        return jnp.stack([ref[:, h * RWKV_HEAD:(h + 1) * RWKV_HEAD]
                          for _ in range(nseq) for h in range(RWKV_HEADS)], axis=0)

    r, k, v, ld, ag = (heads(lambda i, rs=rs: rs[i][...]) for rs in streams)
    lcum = [_dot_exact_lhs(tri, streams[3][i][...]) for i in range(nseq)]
    lc = heads(lambda i: lcum[i])
    kk = k * head_param(kk_ref)
    kk = kk / jnp.maximum(jnp.sqrt(jnp.sum(kk * kk, -1, keepdims=True)), 1e-12)
    kh = k * (1.0 + (ag - 1.0) * head_param(ka_ref))
    b = kk * ag
    lend = lc[:, chunk - 1:chunk, :]
    e_neg = jnp.exp(-lc)
    a_t = -kk * jnp.exp(lc - ld)
    r_t = r * jnp.exp(lc)
    b_t = b * e_neg
    k_t = kh * e_neg
    l_ab = jnp.where(strict, _bmm_nt(a_t, b_t), 0.0)
    l_ak = jnp.where(strict, _bmm_nt(a_t, k_t), 0.0)
    t_rb = jnp.where(incl, _bmm_nt(r_t, b_t), 0.0)
    t_rk = jnp.where(incl, _bmm_nt(r_t, k_t), 0.0)
    s_old = s_ref[...]
    rhs = _bmm_nt(a_t, s_old) + _bmm(l_ak, v)
    inv = eye + l_ab
    lp = l_ab
    n = 2
    while n < chunk:
        lp = _bmm(lp, lp)
        inv = inv + _bmm(inv, lp)
        n *= 2
    u = _bmm(inv, rhs)
    y = _bmm_nt(r_t, s_old) + _bmm(t_rb, u) + _bmm(t_rk, v)
    e_end = jnp.exp(lend - lc)
    uv_t = jnp.swapaxes(jnp.concatenate([u, v], axis=1), 1, 2)
    bke = jnp.concatenate([b * e_end, kh * e_end], axis=1)
    s_ref[...] = s_old * jnp.exp(lend) + _bmm(uv_t, bke)
    mu_y = jnp.mean(y, -1, keepdims=True)
    yc = y - mu_y
    var_y = jnp.mean(yc * yc, -1, keepdims=True)
    yn = yc * lax.rsqrt(var_y + GN_EPS) * head_param(gg_ref) + head_param(gb_ref)
    out = yn + jnp.sum(r * kh * head_param(rk_ref), -1, keepdims=True) * v
    for i in range(nseq):
        for h in range(RWKV_HEADS):
            y_ref[i, :, h * RWKV_HEAD:(h + 1) * RWKV_HEAD] = out[i * RWKV_HEADS + h]

    @pl.when(c == pl.num_programs(1) - 1)
    def _():
        sout_ref[...] = s_ref[...].reshape(sout_ref.shape)


def _wkv(streams, n, t, s0, k_k, k_a, r_k, gn_g, gn_b, chunk, nseq):
    d = streams[0].shape[1]
    nc = t // chunk
    seq_in = lambda s: pl.BlockSpec((chunk, d), lambda i, c: ((i * nseq + s) * nc + c, 0))
    seq = pl.BlockSpec((nseq, chunk, d), lambda i, c: (i, c, 0))
    par = pl.BlockSpec((1, d), lambda i, c: (0, 0))
    st = pl.BlockSpec((nseq, RWKV_HEADS, RWKV_HEAD, RWKV_HEAD), lambda i, c: (i, 0, 0, 0))
    vec = lambda a: a.reshape(1, d)
    r, k, v, ld, ag = ([a] * nseq for a in streams)
    return pl.pallas_call(
        functools.partial(_wkv_kernel, chunk=chunk, nseq=nseq),
        grid=(n // nseq, nc),
        in_specs=[seq_in(s) for _ in range(5) for s in range(nseq)] + [par] * 5 + [st],
        out_specs=[seq, st],
        out_shape=[jax.ShapeDtypeStruct((n, t, d), F32),
                   jax.ShapeDtypeStruct((n, RWKV_HEADS, RWKV_HEAD, RWKV_HEAD), F32)],
        scratch_shapes=[pltpu.VMEM((nseq * RWKV_HEADS, RWKV_HEAD, RWKV_HEAD), F32)],
        compiler_params=_cparams(("parallel", "arbitrary")),
        name="wkv",
    )(*r, *k, *v, *ld, *ag, vec(k_k), vec(k_a), vec(r_k), vec(gn_g), vec(gn_b), s0)


PAGES_PER_SEQ = 16
SEG_PER_PAGE = PAGE_SIZE // CMP_STRIDE
N_SEG = PAGES_PER_SEQ * SEG_PER_PAGE
N_CMP = N_SEG - 1
KV_COLS = KV_HEADS * HEAD_DIM


def _gelu_tanh(x):
    return x * (0.5 * (1.0 + jnp.tanh(np.sqrt(2.0 / np.pi) * (x + 0.044715 * (x * x * x)))))


def _compress_kernel(pt_ref, *refs):
    n_slab = 2 * KV_COLS // LANES
    pages = refs[:PAGES_PER_SEQ * n_slab]
    w1c_ref, w1_ref, pe_ref, b1_ref, w2_ref, b2_ref, kc_ref, vc_ref = refs[PAGES_PER_SEQ * n_slab:]
    valid = lax.broadcasted_iota(jnp.int32, (N_SEG, HEAD_DIM), 0) < N_CMP
    for typ, out_ref in ((0, kc_ref), (1, vc_ref)):
        pe8 = jnp.broadcast_to(pe_ref[typ], (8, CMP_BLOCK * HEAD_DIM))
        c0 = _bdot(pe8, w1_ref[typ])[0:1, :] + b1_ref[typ]
        for pair in range(KV_HEADS // 2):
            cb = typ * (KV_HEADS // 2) + pair
            acc = jnp.zeros((N_SEG, 4 * HEAD_DIM), F32)
            for p in range(CMP_STRIDE):
                xp = jnp.concatenate(
                    [pages[k * n_slab + cb][0, pl.ds(p, SEG_PER_PAGE, stride=CMP_STRIDE), :]
                     for k in range(PAGES_PER_SEQ)], axis=0)
                acc = acc + _bdot(xp, w1c_ref[typ, p])
            for gg in range(2):
                g = pair * 2 + gg
                first = acc[:, gg * LANES:gg * LANES + HEAD_DIM]
                second = acc[:, gg * LANES + HEAD_DIM:(gg + 1) * LANES]
                second_next = pltpu.roll(second, N_SEG - 1, 0)
                hid = _gelu_tanh(first + second_next + c0)
                out = _bdot(hid, w2_ref[typ]) + b2_ref[typ]
                out_ref[0, :, g * HEAD_DIM:(g + 1) * HEAD_DIM] = jnp.where(valid, out, 0.0)


def _compress(pages, layer, page_ids, n_seq, pe, w1, b1, w2, b2):
    w1c = w1.reshape(2, 2, CMP_STRIDE, HEAD_DIM, HEAD_DIM).transpose(0, 2, 3, 1, 4).reshape(
        2, CMP_STRIDE, HEAD_DIM, 2 * HEAD_DIM)
    zero = jnp.zeros_like(w1c)
    w1c = jnp.concatenate([jnp.concatenate([w1c, zero], axis=3), jnp.concatenate([zero, w1c], axis=3)],
                          axis=2)
    n_slab = 2 * KV_COLS // LANES
    slab0 = layer * (4 * KV_COLS // LANES)
    page_spec = lambda k, cb: pl.BlockSpec((1, PAGE_SIZE, LANES),
                                           lambda b, pt: (pt[b * PAGES_PER_SEQ + k], 0, slab0 + cb))
    full = lambda a: pl.BlockSpec(a.shape, lambda b, pt: (0,) * a.ndim)
    consts = (w1c, w1, pe.reshape(2, 1, CMP_BLOCK * HEAD_DIM), b1.reshape(2, 1, HEAD_DIM), w2,
              b2.reshape(2, 1, HEAD_DIM))
    out_spec = pl.BlockSpec((1, N_SEG, KV_COLS), lambda b, pt: (b, 0, 0))
    grid_spec = pltpu.PrefetchScalarGridSpec(
        num_scalar_prefetch=1,
        grid=(n_seq,),
        in_specs=([page_spec(k, cb) for k in range(PAGES_PER_SEQ) for cb in range(n_slab)]
                  + [full(a) for a in consts]),
        out_specs=[out_spec, out_spec],
    )
    return pl.pallas_call(
        _compress_kernel,
        grid_spec=grid_spec,
        out_shape=[jax.ShapeDtypeStruct((n_seq, N_SEG, KV_COLS), F32)] * 2,
        compiler_params=_cparams(("parallel",)),
        name="nsa_compress",
    )(page_ids, *([pages] * (PAGES_PER_SEQ * n_slab)), *consts)


def _overlap_matrix(n_blk):
    c0 = np.arange(N_CMP)[:, None] * CMP_STRIDE
    c1 = c0 + CMP_BLOCK - 1
    s0 = np.arange(n_blk)[None, :] * SEL_BLOCK
    s1 = s0 + SEL_BLOCK - 1
    ov = np.zeros((N_SEG, LANES), np.float32)
    ov[:N_CMP, :n_blk] = (c0 <= s1) & (c1 >= s0)
    return jnp.asarray(ov, BF16)


def _expand_matrix(n_keys):
    ex = (np.arange(LANES)[:, None] == (np.arange(n_keys)[None, :] // SEL_BLOCK)).astype(np.float32)
    return jnp.asarray(ex, BF16)


def _masked_softmax(s, mask):
    s = jnp.where(mask, s, NEG)
    m = jnp.max(s, -1, keepdims=True)
    e = jnp.where(mask, jnp.exp(s - m), 0.0)
    return e / jnp.maximum(jnp.sum(e, -1, keepdims=True), 1e-30)


def _select_blocks(imp, qblk, n_blk):
    lane = lax.broadcasted_iota(jnp.int32, imp.shape, 1)
    forced = (lane == 0) | (lane == qblk) | (lane == qblk - 1)
    imp = jnp.where(forced, jnp.inf, imp)
    imp = jnp.where(lane <= qblk, imp, -jnp.inf)
    rank = jnp.zeros(imp.shape, F32)
    for s in range(n_blk):
        c = imp[:, s:s + 1]
        tie_ahead = jnp.where(lane > s, 1.0, 0.0)
        rank = rank + jnp.where(c > imp, 1.0, jnp.where(c == imp, tie_ahead, 0.0))
    return jnp.where(rank < N_SEL, jnp.where(imp > -jnp.inf, 1.0, 0.0), 0.0)


def _select_blocks_t(imp, qblk, n_blk):
    blk = lax.broadcasted_iota(jnp.int32, imp.shape, 0)
    forced = (blk == 0) | (blk == qblk) | (blk == qblk - 1)
    imp = jnp.where(forced, jnp.inf, imp)
    imp = jnp.where(blk <= qblk, imp, -jnp.inf)
    rank = jnp.zeros(imp.shape, F32)
    for s in range(n_blk):
        c = imp[s:s + 1, :]
        tie_ahead = jnp.where(blk > s, 1.0, 0.0)
        rank = rank + jnp.where(c > imp, 1.0, jnp.where(c == imp, tie_ahead, 0.0))
    return jnp.where(rank < N_SEL, jnp.where(imp > -jnp.inf, 1.0, 0.0), 0.0)


def _softmax_rows_masked(s, mask):
    s = jnp.where(mask, s, NEG)
    m = jnp.max(s, 0, keepdims=True)
    e = jnp.where(mask, jnp.exp(s - m), 0.0)
    return e / jnp.maximum(jnp.sum(e, 0, keepdims=True), 1e-30)


def _nsa_prompt_kernel(q_ref, gt_ref, kc_ref, vc_ref, ks_ref, vs_ref, kw_ref, vw_ref, ovt_ref,
                       o_ref, qt_ref, mx_ref, m_ref, l_ref, acc_ref, oc_ref, *, n_chunks):
    qi = pl.program_id(2)
    scale = HEAD_DIM ** -0.5
    krow = lax.broadcasted_iota(jnp.int32, (QB, QB), 0)
    qcol = lax.broadcasted_iota(jnp.int32, (QB, QB), 1)
    qpos = qi * QB + qcol
    heads4 = lambda a: jnp.concatenate([a] * HPG, axis=1)

    m_ref[...] = jnp.full(m_ref.shape, NEG, F32)
    l_ref[...] = jnp.zeros(l_ref.shape, F32)
    acc_ref[...] = jnp.zeros(acc_ref.shape, F32)

    q_t = q_ref[...].T
    gt_t = jax.nn.sigmoid(gt_ref[...]).T
    cmp_seen = (krow < N_CMP) & (krow * CMP_STRIDE + (CMP_BLOCK - 1) <= qpos)
    mask_c = heads4(jnp.where(cmp_seen, 1.0, 0.0)) > 0.5
    qblk = (qi * QB + lax.broadcasted_iota(jnp.int32, (1, QB), 1)) // SEL_BLOCK
    n_blk = n_chunks * QB // SEL_BLOCK

    for g in range(2):
        for h in range(HPG):
            r0 = (g * HPG + h) * HEAD_DIM
            qt_ref[g, :, h * QB:(h + 1) * QB] = (q_t[r0:r0 + HEAD_DIM, :] * scale).astype(BF16)
        gs = slice(g * HEAD_DIM, (g + 1) * HEAD_DIM)
        p = _softmax_rows_masked(_bdot(kc_ref[0, :, gs], qt_ref[g]), mask_c)
        oc_ref[g] = _bdot_tn(vc_ref[0, :, gs], p)
        p_heads = p[:, 0:QB] + p[:, QB:2 * QB] + p[:, 2 * QB:3 * QB] + p[:, 3 * QB:4 * QB]
        imp = _dot_exact_lhs(ovt_ref[...], p_heads)
        sel = _select_blocks_t(imp[0:n_blk], qblk, n_blk)
        per_step = KEY_STEP // SEL_BLOCK
        for c in range(n_chunks * QB // KEY_STEP):
            mx_ref[g, c] = jnp.concatenate(
                [jnp.broadcast_to(sel[per_step * c + j:per_step * c + j + 1, :], (SEL_BLOCK, QB))
                 for j in range(per_step)], axis=0)

    def online_update(br, g, k, v, mask01):
        valid = heads4(mask01) > 0.5
        s = jnp.where(valid, _bdot(k, qt_ref[g]), NEG)
        m_old = m_ref[br, g]
        m_new = jnp.maximum(m_old, jnp.max(s, 0, keepdims=True))
        alpha = jnp.exp(m_old - m_new)
        p = jnp.where(valid, jnp.exp(s - m_new), 0.0)
        l_ref[br, g] = alpha * l_ref[br, g] + jnp.sum(p, 0, keepdims=True)
        acc_ref[br, g] = alpha * acc_ref[br, g] + _bdot_tn(v, p)
        m_ref[br, g] = m_new

    kstep = lax.broadcasted_iota(jnp.int32, (KEY_STEP, QB), 0)
    qstep = qi * QB + lax.broadcasted_iota(jnp.int32, (KEY_STEP, QB), 1)
    q_end = (qi + 1) * QB

    def sel_body(kc, carry):
        off = pl.multiple_of(kc * KEY_STEP, KEY_STEP)
        causal = off + kstep <= qstep
        for g in range(2):
            k = ks_ref[pl.ds(off, KEY_STEP), pl.ds(g * HEAD_DIM, HEAD_DIM)]
            v = vs_ref[pl.ds(off, KEY_STEP), pl.ds(g * HEAD_DIM, HEAD_DIM)]
            online_update(0, g, k, v, jnp.where(causal, mx_ref[g, kc], 0.0))
        return carry

    n_steps = (q_end + KEY_STEP - 1) // KEY_STEP
    lax.fori_loop(0, n_steps, sel_body, 0)

    def win_body(kc, carry):
        off = pl.multiple_of(kc * KEY_STEP, KEY_STEP)
        dist = qstep - (off + kstep)
        band = jnp.where((dist >= 0) & (dist < WINDOW), 1.0, 0.0)
        for g in range(2):
            k = kw_ref[pl.ds(off, KEY_STEP), pl.ds(g * HEAD_DIM, HEAD_DIM)]
            v = vw_ref[pl.ds(off, KEY_STEP), pl.ds(g * HEAD_DIM, HEAD_DIM)]
            online_update(1, g, k, v, band)
        return carry

    lax.fori_loop(jnp.maximum(qi * QB - WINDOW, 0) // KEY_STEP, n_steps, win_body, 0)

    pieces = []
    for g in range(2):
        o_sel = acc_ref[0, g] / jnp.maximum(l_ref[0, g], 1e-30)
        o_win = acc_ref[1, g] / jnp.maximum(l_ref[1, g], 1e-30)
        o_cmp = oc_ref[g]
        for h in range(HPG):
            cols = slice(h * QB, (h + 1) * QB)
            gc = (g * HPG + h) * 3
            pieces.append(gt_t[gc:gc + 1, :] * o_cmp[:, cols] + gt_t[gc + 1:gc + 2, :] * o_sel[:, cols]
                          + gt_t[gc + 2:gc + 3, :] * o_win[:, cols])
    o_ref[...] = jnp.concatenate(pieces, axis=0).T


def _nsa_prompt(q, gates, kc, vc, rows, win, b, t):
    nq = t // QB
    pair = 2 * HEAD_DIM
    blk = lambda shape, fn: pl.BlockSpec(shape, fn)
    seq_cols = lambda cb: blk((t, pair), lambda i, gp, qi: (i, cb + gp))
    return pl.pallas_call(
        functools.partial(_nsa_prompt_kernel, n_chunks=nq),
        grid=(b, KV_HEADS // 2, nq),
        in_specs=[blk((QB, 2 * HPG * HEAD_DIM), lambda i, gp, qi: (i * nq + qi, gp)),
                  blk((QB, LANES), lambda i, gp, qi: (i * nq + qi, gp)),
                  blk((1, N_SEG, pair), lambda i, gp, qi: (i, 0, gp)),
                  blk((1, N_SEG, pair), lambda i, gp, qi: (i, 0, gp)),
                  seq_cols(4), seq_cols(6),
                  seq_cols(0), seq_cols(2),
                  blk((LANES, N_SEG), lambda i, gp, qi: (0, 0))],
        out_specs=blk((QB, 2 * HPG * HEAD_DIM), lambda i, gp, qi: (i * nq + qi, gp)),
        out_shape=jax.ShapeDtypeStruct((b * t, N_HEADS * HEAD_DIM), F32),
        scratch_shapes=[pltpu.VMEM((2, HEAD_DIM, HPG * QB), BF16),
                        pltpu.VMEM((2, t // KEY_STEP, KEY_STEP, QB), F32),
                        pltpu.VMEM((2, 2, 1, HPG * QB), F32),
                        pltpu.VMEM((2, 2, 1, HPG * QB), F32),
                        pltpu.VMEM((2, 2, HEAD_DIM, HPG * QB), F32),
                        pltpu.VMEM((2, HEAD_DIM, HPG * QB), F32)],
        compiler_params=_cparams(("parallel", "parallel", "arbitrary")),
        name="nsa_prompt",
    )(q, gates, kc, vc, rows, rows, win, win, _overlap_matrix(t // SEL_BLOCK).T)


TQ = 8


def _nsa_sample_kernel(pt_ref, *refs, past_len, n_new):
    pages = refs[:PAGES_PER_SEQ]
    q_ref, gt_ref, kc_ref, vc_ref, new_ref, wb_ref, wn_ref, ov_ref, ex_ref, o_ref = refs[PAGES_PER_SEQ:]
    scale = HEAD_DIM ** -0.5
    rows = HPG * TQ
    tok1 = lax.broadcasted_iota(jnp.int32, (TQ, 1), 0)
    qpos1 = past_len + tok1
    lane = lax.broadcasted_iota(jnp.int32, (TQ, LANES), 1)
    mask_c = (lane < N_CMP) & (lane * CMP_STRIDE + (CMP_BLOCK - 1) <= qpos1)
    newj = lax.broadcasted_iota(jnp.int32, (TQ, TQ), 1)
    newt = lax.broadcasted_iota(jnp.int32, (TQ, TQ), 0)
    mask_new = (newj < n_new) & (newj <= newt)
    wcache = wb_ref.shape[1]
    wi = lax.broadcasted_iota(jnp.int32, (TQ, wcache), 1)
    wpos = past_len - wcache + wi
    mask_w = (wpos <= qpos1) & (wpos > qpos1 - WINDOW)
    n_blk = -(-(past_len + n_new) // SEL_BLOCK)
    gt = jax.nn.sigmoid(gt_ref[0])

    def bcast(mask):
        return jnp.broadcast_to(mask[None], (HPG,) + mask.shape)

    def two_part_attention(s_a, mask_a, s_b, mask_b):
        s_a = jnp.where(mask_a, s_a, NEG)
        s_b = jnp.where(mask_b, s_b, NEG)
        m = jnp.maximum(jnp.max(s_a, -1, keepdims=True), jnp.max(s_b, -1, keepdims=True))
        e_a = jnp.where(mask_a, jnp.exp(s_a - m), 0.0)
        e_b = jnp.where(mask_b, jnp.exp(s_b - m), 0.0)
        den = jnp.maximum(jnp.sum(e_a, -1, keepdims=True) + jnp.sum(e_b, -1, keepdims=True), 1e-30)
        return e_a / den, e_b / den

    for g in range(KV_HEADS):
        gs = slice(g * HEAD_DIM, (g + 1) * HEAD_DIM)
        vs_cols = slice(KV_COLS + g * HEAD_DIM, KV_COLS + (g + 1) * HEAD_DIM)
        qg = jnp.concatenate(
            [q_ref[0, :, (g * HPG + h) * HEAD_DIM:(g * HPG + h + 1) * HEAD_DIM] for h in range(HPG)],
            axis=0) * scale
        qg = qg.astype(BF16)
        s = _bdot_nt(qg, kc_ref[0, :, gs]).reshape(HPG, TQ, N_SEG)
        p = _masked_softmax(s, mask_c[None])
        o_cmp = _bdot(p.reshape(rows, N_SEG), vc_ref[0, :, gs])
        imp = _dot_exact_rhs(p[0] + p[1] + p[2] + p[3], ov_ref[...])
        sel = _select_blocks(imp, qpos1 // SEL_BLOCK, n_blk)
        past_mask = jnp.dot(sel.astype(BF16), ex_ref[...], preferred_element_type=F32) > 0.5
        new_blk = past_len // SEL_BLOCK
        mask_sn = mask_new & (sel[:, new_blk:new_blk + 1] > 0.5)
        s_past = jnp.concatenate([_bdot_nt(qg, pg[0, :, gs]) for pg in pages], axis=1)
        s_new = _bdot_nt(qg, new_ref[0, :, 2 * KV_COLS + g * HEAD_DIM:2 * KV_COLS + (g + 1) * HEAD_DIM])
        p_past, p_new = two_part_attention(s_past.reshape(HPG, TQ, -1), bcast(past_mask),
                                           s_new.reshape(HPG, TQ, TQ), bcast(mask_sn))
        p_past = p_past.reshape(rows, -1)
        o_sel = _bdot(p_new.reshape(rows, TQ),
                      new_ref[0, :, 3 * KV_COLS + g * HEAD_DIM:3 * KV_COLS + (g + 1) * HEAD_DIM])
        for k, pg in enumerate(pages):
            o_sel = o_sel + _bdot(p_past[:, k * PAGE_SIZE:(k + 1) * PAGE_SIZE], pg[0, :, vs_cols])
        s_wb = _bdot_nt(qg, wb_ref[0, :, gs])
        s_wn = _bdot_nt(qg, wn_ref[0, :, gs])
        p_wb, p_wn = two_part_attention(s_wb.reshape(HPG, TQ, wcache), bcast(mask_w),
                                        s_wn.reshape(HPG, TQ, TQ), bcast(mask_new))
        o_win = (_bdot(p_wb.reshape(rows, wcache), wb_ref[0, :, vs_cols])
                 + _bdot(p_wn.reshape(rows, TQ), wn_ref[0, :, vs_cols]))
        for h in range(HPG):
            r8 = slice(h * TQ, (h + 1) * TQ)
            gc = (g // 2) * LANES + ((g % 2) * HPG + h) * 3
            o = (gt[:, gc:gc + 1] * o_cmp[r8] + gt[:, gc + 1:gc + 2] * o_sel[r8]
                 + gt[:, gc + 2:gc + 3] * o_win[r8])
            c0 = (g * HPG + h) * HEAD_DIM
            o_ref[0, :, c0:c0 + HEAD_DIM] = o


def _nsa_sample(pages, layer, page_ids, q, gates, kc, vc, new_rows, win_buf, win_new, past_len, n_new):
    n = q.shape[0]
    wc = win_buf.shape[1]
    n_blk = -(-(past_len + n_new) // SEL_BLOCK)
    page_spec = lambda k: pl.BlockSpec((1, PAGE_SIZE, 2 * KV_COLS),
                                       lambda b, pt: (pt[b * PAGES_PER_SEQ + k], 0, 2 * layer + 1))
    seq = lambda a: pl.BlockSpec((1,) + a.shape[1:], lambda b, pt: (b,) + (0,) * (a.ndim - 1))
    full = lambda a: pl.BlockSpec(a.shape, lambda b, pt: (0,) * a.ndim)
    ov, ex = _overlap_matrix(n_blk), _expand_matrix(past_len)
    per_seq = (q, gates, kc, vc, new_rows, win_buf, win_new)
    grid_spec = pltpu.PrefetchScalarGridSpec(
        num_scalar_prefetch=1,
        grid=(n,),
        in_specs=([page_spec(k) for k in range(PAGES_PER_SEQ)] + [seq(a) for a in per_seq]
                  + [full(ov), full(ex)]),
        out_specs=pl.BlockSpec((1, TQ, N_HEADS * HEAD_DIM), lambda b, pt: (b, 0, 0)),
    )
    return pl.pallas_call(
        functools.partial(_nsa_sample_kernel, past_len=past_len, n_new=n_new),
        grid_spec=grid_spec,
        out_shape=jax.ShapeDtypeStruct((n, TQ, N_HEADS * HEAD_DIM), F32),
        compiler_params=_cparams(("parallel",)),
        name="nsa_sample",
    )(page_ids, *([pages] * PAGES_PER_SEQ), *per_seq, ov, ex)


WKV_SAMPLE_CHUNK = 16
WKV_PROMPT_SEQS = 2
WKV_SAMPLE_SEQS = 8


def _pad_rows(a, n):
    return jnp.pad(a, ((0, 0), (0, n - a.shape[1]), (0, 0)))


def _nsa_layer(x, n_prompt, b, t, db, ds, pages, layer, page_table, win_cache, w_in, w_out, cmp, ln_g,
               ln_b):
    d = x.shape[1]
    qd = N_HEADS * HEAD_DIM
    q = _proj(x, w_in, 512, 0, qd)
    rows = _proj(x, w_in, 512, 2, 4 * KV_COLS)
    win = _proj(x, w_in, 512, 4, 2 * KV_COLS)
    n_gate = 2 * HPG * 3
    g0 = qd + 6 * KV_COLS
    wg = jnp.concatenate(
        [jnp.pad(w_in[:, g0 + p * n_gate:g0 + (p + 1) * n_gate], ((0, 0), (0, LANES - n_gate)))
         for p in range(KV_HEADS // 2)], axis=1)
    gates = _proj(x, wg, wg.shape[1], 0, wg.shape[1])
    past_len = page_table.shape[1] * PAGE_SIZE

    prompt_pages = rows.reshape(rows.shape[0] // PAGE_SIZE, PAGE_SIZE, 4 * KV_COLS)
    kc_p, vc_p = _compress(prompt_pages, 0, jnp.arange(b * PAGES_PER_SEQ, dtype=jnp.int32), b, *cmp)
    o_p = _nsa_prompt(q, gates, kc_p, vc_p, rows, win, b, t)

    q_s, rows_s, win_s, gates_s = (a[n_prompt:].reshape(db, ds, -1) for a in (q, rows, win, gates))
    page_ids = page_table.reshape(-1).astype(jnp.int32)
    kc_s, vc_s = _compress(pages, layer, page_ids, db, *cmp)
    win_buf = win_cache.reshape(db, win_cache.shape[1], 2 * KV_COLS)
    o_s = _nsa_sample(pages, layer, page_ids, _pad_rows(q_s, TQ), _pad_rows(gates_s, TQ), kc_s, vc_s,
                      _pad_rows(rows_s, TQ), win_buf, _pad_rows(win_s, TQ), past_len, ds)[:, :ds]

    o = jnp.concatenate([o_p, o_s.reshape(db * ds, qd)], axis=0)
    x_new = _out_ln(o, None, w_out, x, ln_g, ln_b)
    keep = min(WINDOW, t)
    win_p = win[:n_prompt].reshape(b, t, 2 * KV_COLS)
    outs = (rows[:n_prompt].reshape(b, t, 4, KV_HEADS, HEAD_DIM),
            win_p[:, t - keep:].reshape(b, keep, 2, KV_HEADS, HEAD_DIM),
            rows_s.reshape(db, ds, 4, KV_HEADS, HEAD_DIM),
            jnp.concatenate([win_buf, win_s], axis=1)[:, ds:].reshape(db, -1, 2, KV_HEADS, HEAD_DIM))
    return x_new, outs


def _rwkv_layer(x, n_prompt, b, t, db, ds, shift_s, state_s, mu, w_rkv, w0, w1, w2, a0, a1, a2, g1, g2,
                k_k, k_a, r_k, gn_g, gn_b, w_out, ln_g, ln_b):
    d = x.shape[1]
    xs = x[n_prompt:].reshape(db, ds, d)
    prev_s = jnp.concatenate([shift_s[:, None].astype(x.dtype), xs[:, :-1]], axis=1).reshape(db * ds, d)
    tiles, p_tiles = x.shape[0] // ROW_TILE, n_prompt // ROW_TILE
    last_rows = x[ROW_TILE - 1:n_prompt:ROW_TILE]
    carry = jnp.concatenate([jnp.zeros((1, d), x.dtype), last_rows[:-1]], axis=0)
    seq_start = (jnp.arange(p_tiles) * ROW_TILE) % t == 0
    carry = jnp.where(seq_start[:, None], 0.0, carry)
    carry = jnp.pad(carry, ((0, tiles - p_tiles), (0, 0))).reshape(tiles, 1, d)
    r, k, v, ld, ag, g = _rwkv_pre(x, carry, prev_s, mu, w_rkv, w0, w1, w2, a0, a1, a2, g1, g2)
    streams = (r, k, v, ld, ag)
    head_params = (k_k, k_a, r_k.reshape(d), gn_g, gn_b)
    zero_state = jnp.zeros((b, RWKV_HEADS, RWKV_HEAD, RWKV_HEAD), F32)
    y_p, wkv_p = _wkv(streams, b, t, zero_state, *head_params, WKV_CHUNK, WKV_PROMPT_SEQS)
    streams_s = [_pad_rows(a[n_prompt:].reshape(db, ds, d), WKV_SAMPLE_CHUNK).reshape(-1, d)
                 for a in streams]
    y_s, wkv_s = _wkv(streams_s, db, WKV_SAMPLE_CHUNK, state_s.astype(F32), *head_params,
                      WKV_SAMPLE_CHUNK, WKV_SAMPLE_SEQS)
    y = jnp.concatenate([y_p.reshape(n_prompt, d), y_s[:, :ds].reshape(db * ds, d)], axis=0)
    x_new = _out_ln(y, g, w_out, x, ln_g, ln_b)
    return x_new, (wkv_p, x[t - 1:n_prompt:t], wkv_s.astype(state_s.dtype), xs[:, -1])


def kernel(x_prompt, x_sample, cache_nsa_paged, cache_nsa_win, state_rwkv_wkv, state_rwkv_shift, page_table, nsa_w_in, nsa_w_out, nsa_cmp_pe, nsa_cmp_w1, nsa_cmp_b1, nsa_cmp_w2, nsa_cmp_b2, rwkv_mu, rwkv_w_rkv, rwkv_w0, rwkv_w1, rwkv_w2, rwkv_a0, rwkv_a1, rwkv_a2, rwkv_g1, rwkv_g2, rwkv_k_k, rwkv_k_a, rwkv_r_k, rwkv_gn_g, rwkv_gn_b, rwkv_w_out, ln_g, ln_b, moe_w_router, moe_b_router, moe_w_up, moe_b_up, moe_w_down, moe_b_down):
    b, t, d = x_prompt.shape
    db, ds, _ = x_sample.shape
    n_prompt = b * t
    x = jnp.concatenate([x_prompt.reshape(n_prompt, d), x_sample.reshape(db * ds, d)], axis=0)
    n_phys = cache_nsa_paged.shape[0]
    nsa_outs, rwkv_outs = [], []
    for i in range(DEPTH):
        j = i // 2
        if i % 2 == 0:
            pages = cache_nsa_paged.reshape(n_phys, PAGE_SIZE, -1)
            cmp = (nsa_cmp_pe[j], nsa_cmp_w1[j], nsa_cmp_b1[j], nsa_cmp_w2[j], nsa_cmp_b2[j])
            x, outs = _nsa_layer(x, n_prompt, b, t, db, ds, pages, j, page_table, cache_nsa_win[j],
                                 nsa_w_in[j], nsa_w_out[j], cmp, ln_g[i, 0], ln_b[i, 0])
            nsa_outs.append(outs)
        else:
            x, outs = _rwkv_layer(x, n_prompt, b, t, db, ds, state_rwkv_shift[j], state_rwkv_wkv[j],
                                  rwkv_mu[j], rwkv_w_rkv[j], rwkv_w0[j], rwkv_w1[j], rwkv_w2[j],
                                  rwkv_a0[j], rwkv_a1[j], rwkv_a2[j], rwkv_g1[j], rwkv_g2[j],
                                  rwkv_k_k[j], rwkv_k_a[j], rwkv_r_k[j], rwkv_gn_g[j], rwkv_gn_b[j],
                                  rwkv_w_out[j], ln_g[i, 0], ln_b[i, 0])
            rwkv_outs.append(outs)
        x = _moe_layer(x, moe_w_router[i], moe_b_router[i], moe_w_up[i], moe_b_up[i], moe_w_down[i],
                       moe_b_down[i], ln_g[i, 1], ln_b[i, 1])
    p_rows, p_win, s_rows, s_win = (jnp.stack(z, axis=k) for z, k in zip(zip(*nsa_outs), (2, 0, 2, 0)))
    p_wkv, p_shift, s_wkv, s_shift = (jnp.stack(z, axis=0) for z in zip(*rwkv_outs))
    return (x[:n_prompt].reshape(b, t, d), x[n_prompt:].reshape(db, ds, d),
            p_rows, p_win, p_wkv, p_shift, s_rows, s_win, s_wkv, s_shift)
```

```python
import functools

import numpy as np
import jax
import jax.numpy as jnp
from jax import lax
from jax.experimental import pallas as pl
from jax.experimental.pallas import tpu as pltpu

F32 = jnp.float32
BF16 = jnp.bfloat16

D_MODEL = 1024
DEPTH = 2
N_HEADS = 16
HEAD_DIM = 64
KV_HEADS = 4
HPG = N_HEADS // KV_HEADS
CMP_BLOCK = 32
CMP_STRIDE = 16
SEL_BLOCK = 64
N_SEL = 16
WINDOW = 512
PAGE_SIZE = 128
RWKV_HEAD = 64
RWKV_HEADS = D_MODEL // RWKV_HEAD
GN_EPS = 64e-5
N_EXPERTS = 32
TOP_K = 4
D_EXPERT = D_MODEL
SWIGLU_ALPHA = 1.702
SWIGLU_LIMIT = 7.0
LN_EPS = 1e-5
DN_ALPHA = (2 * DEPTH) ** 0.25

LANES = 128
QB = 128
KEY_STEP = 256
NEG = -1e30
VMEM_LIMIT = 56 * 1024 * 1024
ROW_TILE = 512
MOE_ROWS = 256
WKV_CHUNK = 64


def _cparams(sem):
    return pltpu.CompilerParams(dimension_semantics=sem, vmem_limit_bytes=VMEM_LIMIT)


def _bdot(a, b):
    return jnp.dot(a.astype(BF16), b.astype(BF16), preferred_element_type=F32)


def _bdot_nt(a, b):
    return lax.dot_general(a.astype(BF16), b.astype(BF16), (((1,), (1,)), ((), ())),
                           preferred_element_type=F32)


def _bdot_tn(a, b):
    return lax.dot_general(a.astype(BF16), b.astype(BF16), (((0,), (0,)), ((), ())),
                           preferred_element_type=F32)


def _bmm(a, b):
    return lax.dot_general(a.astype(BF16), b.astype(BF16), (((2,), (1,)), ((0,), (0,))),
                           preferred_element_type=F32)


def _bmm_nt(a, b):
    return lax.dot_general(a.astype(BF16), b.astype(BF16), (((2,), (2,)), ((0,), (0,))),
                           preferred_element_type=F32)


def _split3(x):
    hi = x.astype(BF16)
    r = x - hi.astype(F32)
    mid = r.astype(BF16)
    lo = (r - mid.astype(F32)).astype(BF16)
    return hi, mid, lo


def _dot_exact_rhs(x, w01):
    hi, mid, lo = _split3(x)
    d = lambda a: jnp.dot(a, w01, preferred_element_type=F32)
    return d(hi) + d(mid) + d(lo)


def _dot_exact_lhs(w01, x):
    hi, mid, lo = _split3(x)
    d = lambda a: jnp.dot(w01, a, preferred_element_type=F32)
    return d(hi) + d(mid) + d(lo)


def _layer_norm(z, g, b):
    mu = jnp.mean(z, -1, keepdims=True)
    zc = z - mu
    var = jnp.mean(zc * zc, -1, keepdims=True)
    return zc * lax.rsqrt(var + LN_EPS) * g + b


def _proj_kernel(x_ref, w_ref, o_ref):
    o_ref[...] = _bdot(x_ref[...], w_ref[...])


def _proj(x, w, tn, col_block0, n_out):
    m, k = x.shape
    return pl.pallas_call(
        _proj_kernel,
        grid=(n_out // tn, m // ROW_TILE),
        in_specs=[pl.BlockSpec((ROW_TILE, k), lambda j, i: (i, 0)),
                  pl.BlockSpec((k, tn), lambda j, i: (0, j + col_block0))],
        out_specs=pl.BlockSpec((ROW_TILE, tn), lambda j, i: (i, j)),
        out_shape=jax.ShapeDtypeStruct((m, n_out), F32),
        compiler_params=_cparams(("parallel", "parallel")),
        name="proj",
    )(x, w)


def _out_ln_kernel(*refs, gated):
    if gated:
        a_ref, m_ref, w_ref, res_ref, g_ref, b_ref, o_ref = refs
        a = a_ref[...] * m_ref[...]
    else:
        a_ref, w_ref, res_ref, g_ref, b_ref, o_ref = refs
        a = a_ref[...]
    z = DN_ALPHA * res_ref[...] + _bdot(a, w_ref[...])
    o_ref[...] = _layer_norm(z, g_ref[...], b_ref[...])


def _out_ln(a, mul, w, res, g, b):
    m, k = a.shape
    n = w.shape[1]
    row = lambda i: (i, 0)
    fixed = lambda i: (0, 0)
    ins = [a] + ([mul] if mul is not None else []) + [w, res, g.reshape(1, n), b.reshape(1, n)]
    specs = ([pl.BlockSpec((ROW_TILE, k), row)] * (2 if mul is not None else 1)
             + [pl.BlockSpec((k, n), fixed), pl.BlockSpec((ROW_TILE, n), row),
                pl.BlockSpec((1, n), fixed), pl.BlockSpec((1, n), fixed)])
    return pl.pallas_call(
        functools.partial(_out_ln_kernel, gated=mul is not None),
        grid=(m // ROW_TILE,),
        in_specs=specs,
        out_specs=pl.BlockSpec((ROW_TILE, n), row),
        out_shape=jax.ShapeDtypeStruct((m, n), F32),
        compiler_params=_cparams(("parallel",)),
        name="out_ln",
    )(*ins)


def _router_kernel(x_ref, w_ref, b_ref, e_ref, g_ref):
    xh, xm, xl = _split3(x_ref[...])
    wh, wm, wl = _split3(w_ref[...])
    d = lambda a, c: jnp.dot(a, c, preferred_element_type=F32)
    acc = d(xh, wh) + (d(xh, wm) + d(xm, wh)) + (d(xm, wm) + d(xh, wl) + d(xl, wh))
    lane = lax.broadcasted_iota(jnp.int32, acc.shape, 1).astype(F32)
    logits = jnp.where(lane < N_EXPERTS, acc + b_ref[...], -jnp.inf)
    vals, ids = [], []
    for _ in range(TOP_K):
        m = jnp.max(logits, -1, keepdims=True)
        idx = jnp.min(jnp.where(logits == m, lane, float(LANES)), -1, keepdims=True)
        vals.append(m)
        ids.append(idx)
        logits = jnp.where(lane == idx, -jnp.inf, logits)
    es = [jnp.exp(v - vals[0]) for v in vals]
    den = functools.reduce(lambda a, c: a + c, es)
    e_out = jnp.zeros(acc.shape, F32)
    g_out = jnp.zeros(acc.shape, F32)
    for k in range(TOP_K):
        e_out = jnp.where(lane == k, ids[k], e_out)
        g_out = jnp.where(lane == k, es[k] / den, g_out)
    e_ref[...] = e_out
    g_ref[...] = g_out


def _router(x, w_pad, b_pad):
    m, k = x.shape
    out = pl.BlockSpec((ROW_TILE, LANES), lambda i: (i, 0))
    return pl.pallas_call(
        _router_kernel,
        grid=(m // ROW_TILE,),
        in_specs=[pl.BlockSpec((ROW_TILE, k), lambda i: (i, 0)),
                  pl.BlockSpec((k, LANES), lambda i: (0, 0)),
                  pl.BlockSpec((1, LANES), lambda i: (0, 0))],
        out_specs=[out, out],
        out_shape=[jax.ShapeDtypeStruct((m, LANES), F32)] * 2,
        compiler_params=_cparams(("parallel",)),
        name="router",
    )(x, w_pad, b_pad)


def _moe_kernel(be_ref, nb_ref, x_ref, wu_ref, bu_ref, wd_ref, bd_ref, gw_ref, o_ref, wu_bf, wd_bf):
    i = pl.program_id(0)
    used = i < nb_ref[0]

    @pl.when(used & ((i == 0) | (be_ref[i] != be_ref[jnp.maximum(i - 1, 0)])))
    def _():
        wu_bf[...] = wu_ref[0].astype(BF16)
        wd_bf[...] = wd_ref[0].astype(BF16)

    @pl.when(used)
    def _():
        u = _bdot(x_ref[...], wu_bf[...]) + bu_ref[0]
        glu = jnp.minimum(u[:, :D_EXPERT], SWIGLU_LIMIT)
        lin = jnp.clip(u[:, D_EXPERT:], -SWIGLU_LIMIT, SWIGLU_LIMIT)
        h = glu * jax.nn.sigmoid(SWIGLU_ALPHA * glu) * (lin + 1.0)
        y = _bdot(h, wd_bf[...]) + bd_ref[0]
        o_ref[...] = y * gw_ref[...]

    @pl.when(i >= nb_ref[0])
    def _():
        o_ref[...] = jnp.zeros_like(o_ref)


def _moe_experts(xg, gw, block_e, n_used, w_up, b_up, w_down, b_down):
    r, d = xg.shape
    nblk = r // MOE_ROWS
    e, _, d2 = w_up.shape
    grid_spec = pltpu.PrefetchScalarGridSpec(
        num_scalar_prefetch=2,
        grid=(nblk,),
        in_specs=[pl.BlockSpec((MOE_ROWS, d), lambda i, be, nb: (i, 0)),
                  pl.BlockSpec((1, d, d2), lambda i, be, nb: (be[i], 0, 0)),
                  pl.BlockSpec((1, 1, d2), lambda i, be, nb: (be[i], 0, 0)),
                  pl.BlockSpec((1, d2 // 2, d), lambda i, be, nb: (be[i], 0, 0)),
                  pl.BlockSpec((1, 1, d), lambda i, be, nb: (be[i], 0, 0)),
                  pl.BlockSpec((MOE_ROWS, 1), lambda i, be, nb: (i, 0))],
        out_specs=pl.BlockSpec((MOE_ROWS, d), lambda i, be, nb: (i, 0)),
        scratch_shapes=[pltpu.VMEM((d, d2), BF16), pltpu.VMEM((d2 // 2, d), BF16)],
    )
    return pl.pallas_call(
        _moe_kernel,
        grid_spec=grid_spec,
        out_shape=jax.ShapeDtypeStruct((r, d), F32),
        compiler_params=_cparams(("arbitrary",)),
        name="moe_experts",
    )(block_e, n_used, xg, w_up, b_up.reshape(e, 1, d2), w_down, b_down.reshape(e, 1, d), gw)


def _combine_ln_kernel(res_ref, y_ref, g_ref, b_ref, o_ref):
    y = y_ref[0] + y_ref[1] + y_ref[2] + y_ref[3]
    o_ref[...] = _layer_norm(DN_ALPHA * res_ref[...] + y, g_ref[...], b_ref[...])


def _combine_ln(res, y4, g, b):
    m, n = res.shape
    return pl.pallas_call(
        _combine_ln_kernel,
        grid=(m // ROW_TILE,),
        in_specs=[pl.BlockSpec((ROW_TILE, n), lambda i: (i, 0)),
                  pl.BlockSpec((TOP_K, ROW_TILE, n), lambda i: (0, i, 0)),
                  pl.BlockSpec((1, n), lambda i: (0, 0)),
                  pl.BlockSpec((1, n), lambda i: (0, 0))],
        out_specs=pl.BlockSpec((ROW_TILE, n), lambda i: (i, 0)),
        out_shape=jax.ShapeDtypeStruct((m, n), F32),
        compiler_params=_cparams(("parallel",)),
        name="combine_ln",
    )(res, y4, g.reshape(1, n), b.reshape(1, n))


def _moe_layer(x, w_router, b_router, w_up, b_up, w_down, b_down, ln_g, ln_b):
    n = x.shape[0]
    wr = jnp.pad(w_router, ((0, 0), (0, LANES - N_EXPERTS)))
    br = jnp.pad(b_router, (0, LANES - N_EXPERTS)).reshape(1, LANES)
    ids, gates = _router(x, wr, br)
    top_e = ids[:, :TOP_K].astype(jnp.int32)
    gate = gates[:, :TOP_K]
    n_assign = n * TOP_K
    flat_e = top_e.reshape(-1).astype(jnp.int32)
    iota = jnp.arange(n_assign, dtype=jnp.int32)
    e_sorted, order = lax.sort((flat_e, iota), num_keys=1)
    _, sorted_pos = lax.sort((order, iota), num_keys=1)
    experts = jnp.arange(N_EXPERTS, dtype=jnp.int32)
    counts = jnp.sum((flat_e[None, :] == experts[:, None]).astype(jnp.int32), axis=1)
    raw_start = jnp.cumsum(counts) - counts
    padded = (counts + MOE_ROWS - 1) // MOE_ROWS * MOE_ROWS
    pad_end = jnp.cumsum(padded)
    pad_start = pad_end - padded
    n_blocks = -(-n_assign // MOE_ROWS) + N_EXPERTS
    n_rows = n_blocks * MOE_ROWS
    block_start = jnp.arange(n_blocks, dtype=jnp.int32) * MOE_ROWS
    block_e = jnp.minimum(jnp.sum(pad_end[None, :] <= block_start[:, None], axis=1),
                          N_EXPERTS - 1).astype(jnp.int32)
    n_used = (pad_end[-1:] // MOE_ROWS).astype(jnp.int32)
    j = (block_start - pad_start[block_e])[:, None] + jnp.arange(MOE_ROWS, dtype=jnp.int32)[None, :]
    valid = (j < counts[block_e][:, None]).reshape(-1)
    src = jnp.clip(raw_start[block_e][:, None] + j, 0, n_assign - 1).reshape(-1)
    assign_of_row = order[src]
    tok_of_row = jnp.where(valid, assign_of_row // TOP_K, 0)
    gate_of_row = jnp.where(valid, gate.reshape(-1)[assign_of_row], 0.0)
    row_of_assign = (pad_start - raw_start)[flat_e] + sorted_pos
    xg = x[tok_of_row]
    yb = _moe_experts(xg, gate_of_row.reshape(n_rows, 1), block_e, n_used, w_up, b_up, w_down, b_down)
    y4 = yb[row_of_assign.reshape(n, TOP_K).T]
    return _combine_ln(x, y4, ln_g, ln_b)


def _rwkv_pre_kernel(x_ref, carry_ref, xps_ref, mu_ref, wrkv_ref, w0_ref, w1_ref, w2_ref, a0_ref, a1_ref,
                     a2_ref, g1_ref, g2_ref, r_ref, k_ref, v_ref, ld_ref, ag_ref, g_ref, *, prompt_tiles):
    x = x_ref[...]
    first = lax.broadcasted_iota(jnp.int32, x.shape, 0) == 0
    x_prev = jnp.where(first, carry_ref[0], pltpu.roll(x, 1, 0))
    x_prev = jnp.where(pl.program_id(0) >= prompt_tiles, xps_ref[...], x_prev)
    dx = x_prev - x
    mix = lambda s: x + dx * mu_ref[s:s + 1, :]
    r_ref[...] = _bdot(mix(0), wrkv_ref[0])
    k_ref[...] = _bdot(mix(1), wrkv_ref[1])
    v_ref[...] = _bdot(mix(2), wrkv_ref[2])
    lw = w0_ref[...] + _bdot(jnp.tanh(_bdot(mix(3), w1_ref[...])), w2_ref[...])
    z = -lw
    softplus = jnp.maximum(z, 0.0) + jnp.log1p(jnp.exp(-jnp.abs(z)))
    ld_ref[...] = -jnp.exp(-softplus - 0.5)
    ag_ref[...] = jax.nn.sigmoid(a0_ref[...] + _bdot(_bdot(mix(4), a1_ref[...]), a2_ref[...]))
    g_ref[...] = _bdot(jax.nn.sigmoid(_bdot(mix(5), g1_ref[...])), g2_ref[...])


def _rwkv_pre(x, carry, x_prev_sample, mu, w_rkv, w0, w1, w2, a0, a1, a2, g1, g2):
    m, d = x.shape
    prompt_tiles = (m - x_prev_sample.shape[0]) // ROW_TILE
    row = pl.BlockSpec((ROW_TILE, d), lambda i: (i, 0))
    full = lambda a: pl.BlockSpec(a.shape, lambda i: (0,) * a.ndim)
    w0, a0 = w0.reshape(1, d), a0.reshape(1, d)
    weights = (mu, w_rkv, w0, w1, w2, a0, a1, a2, g1, g2)
    return pl.pallas_call(
        functools.partial(_rwkv_pre_kernel, prompt_tiles=prompt_tiles),
        grid=(m // ROW_TILE,),
        in_specs=[row, pl.BlockSpec((1, 1, d), lambda i: (i, 0, 0)),
                  pl.BlockSpec((ROW_TILE, d), lambda i: (jnp.maximum(i - prompt_tiles, 0), 0))]
                 + [full(a) for a in weights],
        out_specs=[row] * 6,
        out_shape=[jax.ShapeDtypeStruct((m, d), F32)] * 6,
        compiler_params=_cparams(("parallel",)),
        name="rwkv_pre",
    )(x, carry, x_prev_sample, *weights)


def _wkv_kernel(*refs, chunk, nseq):
    streams = [refs[j * nseq:(j + 1) * nseq] for j in range(5)]
    kk_ref, ka_ref, rk_ref, gg_ref, gb_ref, s0_ref, y_ref, sout_ref, s_ref = refs[5 * nseq:]
    c = pl.program_id(1)
    nh = nseq * RWKV_HEADS

    @pl.when(c == 0)
    def _():
        s_ref[...] = s0_ref[...].reshape(nh, RWKV_HEAD, RWKV_HEAD)

    row = lax.broadcasted_iota(jnp.int32, (chunk, chunk), 0)
    col = lax.broadcasted_iota(jnp.int32, (chunk, chunk), 1)
    incl = (col <= row)[None]
    strict = (col < row)[None]
    tri = jnp.where(col <= row, 1.0, 0.0).astype(BF16)
    eye = jnp.where(col == row, 1.0, 0.0).astype(F32)[None]

    def heads(rows_of_seq):
        return jnp.stack([rows_of_seq(i)[:, h * RWKV_HEAD:(h + 1) * RWKV_HEAD]
                          for i in range(nseq) for h in range(RWKV_HEADS)], axis=0)

    def head_param(ref):
        return jnp.stack([ref[:, h * RWKV_HEAD:(h + 1) * RWKV_HEAD]
                          for _ in range(nseq) for h in range(RWKV_HEADS)], axis=0)

    r, k, v, ld, ag = (heads(lambda i, rs=rs: rs[i][...]) for rs in streams)
    lcum = [_dot_exact_lhs(tri, streams[3][i][...]) for i in range(nseq)]
    lc = heads(lambda i: lcum[i])
    kk = k * head_param(kk_ref)
    kk = kk / jnp.maximum(jnp.sqrt(jnp.sum(kk * kk, -1, keepdims=True)), 1e-12)
    kh = k * (1.0 + (ag - 1.0) * head_param(ka_ref))
    b = kk * ag
    lend = lc[:, chunk - 1:chunk, :]
    e_neg = jnp.exp(-lc)
    a_t = -kk * jnp.exp(lc - ld)
    r_t = r * jnp.exp(lc)
    b_t = b * e_neg
    k_t = kh * e_neg
    l_ab = jnp.where(strict, _bmm_nt(a_t, b_t), 0.0)
    l_ak = jnp.where(strict, _bmm_nt(a_t, k_t), 0.0)
    t_rb = jnp.where(incl, _bmm_nt(r_t, b_t), 0.0)
    t_rk = jnp.where(incl, _bmm_nt(r_t, k_t), 0.0)
    s_old = s_ref[...]
    rhs = _bmm_nt(a_t, s_old) + _bmm(l_ak, v)
    inv = eye + l_ab
    lp = l_ab
    n = 2
    while n < chunk:
        lp = _bmm(lp, lp)
        inv = inv + _bmm(inv, lp)
        n *= 2
    u = _bmm(inv, rhs)
    y = _bmm_nt(r_t, s_old) + _bmm(t_rb, u) + _bmm(t_rk, v)
    e_end = jnp.exp(lend - lc)
    uv_t = jnp.swapaxes(jnp.concatenate([u, v], axis=1), 1, 2)
    bke = jnp.concatenate([b * e_end, kh * e_end], axis=1)
    s_ref[...] = s_old * jnp.exp(lend) + _bmm(uv_t, bke)
    mu_y = jnp.mean(y, -1, keepdims=True)
    yc = y - mu_y
    var_y = jnp.mean(yc * yc, -1, keepdims=True)
    yn = yc * lax.rsqrt(var_y + GN_EPS) * head_param(gg_ref) + head_param(gb_ref)
    out = yn + jnp.sum(r * kh * head_param(rk_ref), -1, keepdims=True) * v
    for i in range(nseq):
        for h in range(RWKV_HEADS):
            y_ref[i, :, h * RWKV_HEAD:(h + 1) * RWKV_HEAD] = out[i * RWKV_HEADS + h]

    @pl.when(c == pl.num_programs(1) - 1)
    def _():
        sout_ref[...] = s_ref[...].reshape(sout_ref.shape)


def _wkv(streams, n, t, s0, k_k, k_a, r_k, gn_g, gn_b, chunk, nseq):
    d = streams[0].shape[1]
    nc = t // chunk
    seq_in = lambda s: pl.BlockSpec((chunk, d), lambda i, c: ((i * nseq + s) * nc + c, 0))
    seq = pl.BlockSpec((nseq, chunk, d), lambda i, c: (i, c, 0))
    par = pl.BlockSpec((1, d), lambda i, c: (0, 0))
    st = pl.BlockSpec((nseq, RWKV_HEADS, RWKV_HEAD, RWKV_HEAD), lambda i, c: (i, 0, 0, 0))
    vec = lambda a: a.reshape(1, d)
    r, k, v, ld, ag = ([a] * nseq for a in streams)
    return pl.pallas_call(
        functools.partial(_wkv_kernel, chunk=chunk, nseq=nseq),
        grid=(n // nseq, nc),
        in_specs=[seq_in(s) for _ in range(5) for s in range(nseq)] + [par] * 5 + [st],
        out_specs=[seq, st],
        out_shape=[jax.ShapeDtypeStruct((n, t, d), F32),
                   jax.ShapeDtypeStruct((n, RWKV_HEADS, RWKV_HEAD, RWKV_HEAD), F32)],
        scratch_shapes=[pltpu.VMEM((nseq * RWKV_HEADS, RWKV_HEAD, RWKV_HEAD), F32)],
        compiler_params=_cparams(("parallel", "arbitrary")),
        name="wkv",
    )(*r, *k, *v, *ld, *ag, vec(k_k), vec(k_a), vec(r_k), vec(gn_g), vec(gn_b), s0)


PAGES_PER_SEQ = 16
SEG_PER_PAGE = PAGE_SIZE // CMP_STRIDE
N_SEG = PAGES_PER_SEQ * SEG_PER_PAGE
N_CMP = N_SEG - 1
KV_COLS = KV_HEADS * HEAD_DIM


def _gelu_tanh(x):
    return x * (0.5 * (1.0 + jnp.tanh(np.sqrt(2.0 / np.pi) * (x + 0.044715 * (x * x * x)))))


def _compress_kernel(pt_ref, *refs):
    n_slab = 2 * KV_COLS // LANES
    pages = refs[:PAGES_PER_SEQ * n_slab]
    w1c_ref, w1_ref, pe_ref, b1_ref, w2_ref, b2_ref, kc_ref, vc_ref = refs[PAGES_PER_SEQ * n_slab:]
    valid = lax.broadcasted_iota(jnp.int32, (N_SEG, HEAD_DIM), 0) < N_CMP
    for typ, out_ref in ((0, kc_ref), (1, vc_ref)):
        pe8 = jnp.broadcast_to(pe_ref[typ], (8, CMP_BLOCK * HEAD_DIM))
        c0 = _bdot(pe8, w1_ref[typ])[0:1, :] + b1_ref[typ]
        for pair in range(KV_HEADS // 2):
            cb = typ * (KV_HEADS // 2) + pair
            acc = jnp.zeros((N_SEG, 4 * HEAD_DIM), F32)
            rows_at = lambda p: jnp.concatenate(
                [pages[k * n_slab + cb][0, pl.ds(p, SEG_PER_PAGE, stride=CMP_STRIDE), :]
                 for k in range(PAGES_PER_SEQ)], axis=0)
            for p in range(0, CMP_STRIDE, 2):
                xp = jnp.concatenate([rows_at(p), rows_at(p + 1)], axis=1)
                acc = acc + _bdot(xp, w1c_ref[typ, p // 2])
            for gg in range(2):
                g = pair * 2 + gg
                first = acc[:, gg * LANES:gg * LANES + HEAD_DIM]
                second = acc[:, gg * LANES + HEAD_DIM:(gg + 1) * LANES]
                second_next = pltpu.roll(second, N_SEG - 1, 0)
                hid = _gelu_tanh(first + second_next + c0)
                out = _bdot(hid, w2_ref[typ]) + b2_ref[typ]
                out_ref[0, :, g * HEAD_DIM:(g + 1) * HEAD_DIM] = jnp.where(valid, out, 0.0)


def _compress(pages, layer, page_ids, n_seq, pe, w1, b1, w2, b2):
    w1c = w1.reshape(2, 2, CMP_STRIDE, HEAD_DIM, HEAD_DIM).transpose(0, 2, 3, 1, 4).reshape(
        2, CMP_STRIDE, HEAD_DIM, 2 * HEAD_DIM)
    zero = jnp.zeros_like(w1c)
    w1c = jnp.concatenate([jnp.concatenate([w1c, zero], axis=3), jnp.concatenate([zero, w1c], axis=3)],
                          axis=2)
    w1c = w1c.reshape(2, CMP_STRIDE // 2, 2 * LANES, 4 * HEAD_DIM)
    n_slab = 2 * KV_COLS // LANES
    slab0 = layer * (4 * KV_COLS // LANES)
    page_spec = lambda k, cb: pl.BlockSpec((1, PAGE_SIZE, LANES),
                                           lambda b, pt: (pt[b * PAGES_PER_SEQ + k], 0, slab0 + cb))
    full = lambda a: pl.BlockSpec(a.shape, lambda b, pt: (0,) * a.ndim)
    consts = (w1c, w1, pe.reshape(2, 1, CMP_BLOCK * HEAD_DIM), b1.reshape(2, 1, HEAD_DIM), w2,
              b2.reshape(2, 1, HEAD_DIM))
    out_spec = pl.BlockSpec((1, N_SEG, KV_COLS), lambda b, pt: (b, 0, 0))
    grid_spec = pltpu.PrefetchScalarGridSpec(
        num_scalar_prefetch=1,
        grid=(n_seq,),
        in_specs=([page_spec(k, cb) for k in range(PAGES_PER_SEQ) for cb in range(n_slab)]
                  + [full(a) for a in consts]),
        out_specs=[out_spec, out_spec],
    )
    return pl.pallas_call(
        _compress_kernel,
        grid_spec=grid_spec,
        out_shape=[jax.ShapeDtypeStruct((n_seq, N_SEG, KV_COLS), F32)] * 2,
        compiler_params=_cparams(("parallel",)),
        name="nsa_compress",
    )(page_ids, *([pages] * (PAGES_PER_SEQ * n_slab)), *consts)


def _overlap_matrix(n_blk):
    c0 = np.arange(N_CMP)[:, None] * CMP_STRIDE
    c1 = c0 + CMP_BLOCK - 1
    s0 = np.arange(n_blk)[None, :] * SEL_BLOCK
    s1 = s0 + SEL_BLOCK - 1
    ov = np.zeros((N_SEG, LANES), np.float32)
    ov[:N_CMP, :n_blk] = (c0 <= s1) & (c1 >= s0)
    return jnp.asarray(ov, BF16)


def _expand_matrix(n_keys):
    ex = (np.arange(LANES)[:, None] == (np.arange(n_keys)[None, :] // SEL_BLOCK)).astype(np.float32)
    return jnp.asarray(ex, BF16)


def _masked_softmax(s, mask):
    s = jnp.where(mask, s, NEG)
    m = jnp.max(s, -1, keepdims=True)
    e = jnp.where(mask, jnp.exp(s - m), 0.0)
    return e / jnp.maximum(jnp.sum(e, -1, keepdims=True), 1e-30)


def _select_blocks(imp, qblk, n_blk):
    lane = lax.broadcasted_iota(jnp.int32, imp.shape, 1)
    forced = (lane == 0) | (lane == qblk) | (lane == qblk - 1)
    imp = jnp.where(forced, jnp.inf, imp)
    imp = jnp.where(lane <= qblk, imp, -jnp.inf)
    rank = jnp.zeros(imp.shape, F32)
    for s in range(n_blk):
        c = imp[:, s:s + 1]
        tie_ahead = jnp.where(lane > s, 1.0, 0.0)
        rank = rank + jnp.where(c > imp, 1.0, jnp.where(c == imp, tie_ahead, 0.0))
    return jnp.where(rank < N_SEL, jnp.where(imp > -jnp.inf, 1.0, 0.0), 0.0)


def _select_blocks_t(imp, qblk, n_blk):
    blk = lax.broadcasted_iota(jnp.int32, imp.shape, 0)
    forced = (blk == 0) | (blk == qblk) | (blk == qblk - 1)
    imp = jnp.where(forced, jnp.inf, imp)
    imp = jnp.where(blk <= qblk, imp, -jnp.inf)
    rank = jnp.zeros(imp.shape, F32)
    for s in range(n_blk):
        c = imp[s:s + 1, :]
        tie_ahead = jnp.where(blk > s, 1.0, 0.0)
        rank = rank + jnp.where(c > imp, 1.0, jnp.where(c == imp, tie_ahead, 0.0))
    return jnp.where(rank < N_SEL, jnp.where(imp > -jnp.inf, 1.0, 0.0), 0.0)


def _softmax_rows_masked(s, mask):
    s = jnp.where(mask, s, NEG)
    m = jnp.max(s, 0, keepdims=True)
    e = jnp.where(mask, jnp.exp(s - m), 0.0)
    return e / jnp.maximum(jnp.sum(e, 0, keepdims=True), 1e-30)


def _nsa_prompt_kernel(q_ref, gt_ref, kc_ref, vc_ref, ks_ref, vs_ref, kw_ref, vw_ref, ovt_ref,
                       o_ref, qt_ref, mx_ref, m_ref, l_ref, acc_ref, oc_ref, *, n_chunks):
    qi = pl.program_id(2)
    scale = HEAD_DIM ** -0.5
    krow = lax.broadcasted_iota(jnp.int32, (QB, QB), 0)
    qcol = lax.broadcasted_iota(jnp.int32, (QB, QB), 1)
    qpos = qi * QB + qcol
    heads4 = lambda a: jnp.concatenate([a] * HPG, axis=1)

    m_ref[...] = jnp.full(m_ref.shape, NEG, F32)
    l_ref[...] = jnp.zeros(l_ref.shape, F32)
    acc_ref[...] = jnp.zeros(acc_ref.shape, F32)

    q_t = q_ref[...].T
    gt_t = jax.nn.sigmoid(gt_ref[...]).T
    cmp_seen = (krow < N_CMP) & (krow * CMP_STRIDE + (CMP_BLOCK - 1) <= qpos)
    mask_c = heads4(jnp.where(cmp_seen, 1.0, 0.0)) > 0.5
    qblk = (qi * QB + lax.broadcasted_iota(jnp.int32, (1, QB), 1)) // SEL_BLOCK
    n_blk = n_chunks * QB // SEL_BLOCK

    for g in range(2):
        for h in range(HPG):
            r0 = (g * HPG + h) * HEAD_DIM
            qt_ref[g, :, h * QB:(h + 1) * QB] = (q_t[r0:r0 + HEAD_DIM, :] * scale).astype(BF16)
        gs = slice(g * HEAD_DIM, (g + 1) * HEAD_DIM)
        p = _softmax_rows_masked(_bdot(kc_ref[0, :, gs], qt_ref[g]), mask_c)
        oc_ref[g] = _bdot_tn(vc_ref[0, :, gs], p)
        p_heads = p[:, 0:QB] + p[:, QB:2 * QB] + p[:, 2 * QB:3 * QB] + p[:, 3 * QB:4 * QB]
        imp = _dot_exact_lhs(ovt_ref[...], p_heads)
        sel = _select_blocks_t(imp[0:n_blk], qblk, n_blk)
        per_step = KEY_STEP // SEL_BLOCK
        for c in range(n_chunks * QB // KEY_STEP):
            mx_ref[g, c] = jnp.concatenate(
                [jnp.broadcast_to(sel[per_step * c + j:per_step * c + j + 1, :], (SEL_BLOCK, QB))
                 for j in range(per_step)], axis=0)

    def online_update(br, g, k, v, mask01):
        valid = heads4(mask01) > 0.5
        s = jnp.where(valid, _bdot(k, qt_ref[g]), NEG)
        m_old = m_ref[br, g]
        m_new = jnp.maximum(m_old, jnp.max(s, 0, keepdims=True))
        alpha = jnp.exp(m_old - m_new)
        p = jnp.where(valid, jnp.exp(s - m_new), 0.0)
        l_ref[br, g] = alpha * l_ref[br, g] + jnp.sum(p, 0, keepdims=True)
        acc_ref[br, g] = alpha * acc_ref[br, g] + _bdot_tn(v, p)
        m_ref[br, g] = m_new

    kstep = lax.broadcasted_iota(jnp.int32, (KEY_STEP, QB), 0)
    qstep = qi * QB + lax.broadcasted_iota(jnp.int32, (KEY_STEP, QB), 1)
    q_end = (qi + 1) * QB

    def sel_body(kc, carry):
        off = pl.multiple_of(kc * KEY_STEP, KEY_STEP)
        causal = off + kstep <= qstep
        for g in range(2):
            k = ks_ref[pl.ds(off, KEY_STEP), pl.ds(g * HEAD_DIM, HEAD_DIM)]
            v = vs_ref[pl.ds(off, KEY_STEP), pl.ds(g * HEAD_DIM, HEAD_DIM)]
            online_update(0, g, k, v, jnp.where(causal, mx_ref[g, kc], 0.0))
        return carry

    n_steps = (q_end + KEY_STEP - 1) // KEY_STEP
    lax.fori_loop(0, n_steps, sel_body, 0)

    def win_body(kc, carry):
        off = pl.multiple_of(kc * KEY_STEP, KEY_STEP)
        dist = qstep - (off + kstep)
        band = jnp.where((dist >= 0) & (dist < WINDOW), 1.0, 0.0)
        for g in range(2):
            k = kw_ref[pl.ds(off, KEY_STEP), pl.ds(g * HEAD_DIM, HEAD_DIM)]
            v = vw_ref[pl.ds(off, KEY_STEP), pl.ds(g * HEAD_DIM, HEAD_DIM)]
            online_update(1, g, k, v, band)
        return carry

    lax.fori_loop(jnp.maximum(qi * QB - WINDOW, 0) // KEY_STEP, n_steps, win_body, 0)

    pieces = []
    for g in range(2):
        o_sel = acc_ref[0, g] / jnp.maximum(l_ref[0, g], 1e-30)
        o_win = acc_ref[1, g] / jnp.maximum(l_ref[1, g], 1e-30)
        o_cmp = oc_ref[g]
        for h in range(HPG):
            cols = slice(h * QB, (h + 1) * QB)
            gc = (g * HPG + h) * 3
            pieces.append(gt_t[gc:gc + 1, :] * o_cmp[:, cols] + gt_t[gc + 1:gc + 2, :] * o_sel[:, cols]
                          + gt_t[gc + 2:gc + 3, :] * o_win[:, cols])
    o_ref[...] = jnp.concatenate(pieces, axis=0).T


def _nsa_prompt(q, gates, kc, vc, rows, win, b, t):
    nq = t // QB
    pair = 2 * HEAD_DIM
    blk = lambda shape, fn: pl.BlockSpec(shape, fn)
    seq_cols = lambda cb: blk((t, pair), lambda i, gp, qi: (i, cb + gp))
    return pl.pallas_call(
        functools.partial(_nsa_prompt_kernel, n_chunks=nq),
        grid=(b, KV_HEADS // 2, nq),
        in_specs=[blk((QB, 2 * HPG * HEAD_DIM), lambda i, gp, qi: (i * nq + qi, gp)),
                  blk((QB, LANES), lambda i, gp, qi: (i * nq + qi, gp)),
                  blk((1, N_SEG, pair), lambda i, gp, qi: (i, 0, gp)),
                  blk((1, N_SEG, pair), lambda i, gp, qi: (i, 0, gp)),
                  seq_cols(4), seq_cols(6),
                  seq_cols(0), seq_cols(2),
                  blk((LANES, N_SEG), lambda i, gp, qi: (0, 0))],
        out_specs=blk((QB, 2 * HPG * HEAD_DIM), lambda i, gp, qi: (i * nq + qi, gp)),
        out_shape=jax.ShapeDtypeStruct((b * t, N_HEADS * HEAD_DIM), F32),
        scratch_shapes=[pltpu.VMEM((2, HEAD_DIM, HPG * QB), BF16),
                        pltpu.VMEM((2, t // KEY_STEP, KEY_STEP, QB), F32),
                        pltpu.VMEM((2, 2, 1, HPG * QB), F32),
                        pltpu.VMEM((2, 2, 1, HPG * QB), F32),
                        pltpu.VMEM((2, 2, HEAD_DIM, HPG * QB), F32),
                        pltpu.VMEM((2, HEAD_DIM, HPG * QB), F32)],
        compiler_params=_cparams(("parallel", "parallel", "arbitrary")),
        name="nsa_prompt",
    )(q, gates, kc, vc, rows, rows, win, win, _overlap_matrix(t // SEL_BLOCK).T)


TQ = 8


def _nsa_sample_kernel(pt_ref, *refs, past_len, n_new):
    pages = refs[:PAGES_PER_SEQ]
    q_ref, gt_ref, kc_ref, vc_ref, new_ref, wb_ref, wn_ref, ov_ref, ex_ref, o_ref = refs[PAGES_PER_SEQ:]
    scale = HEAD_DIM ** -0.5
    rows = HPG * TQ
    tok1 = lax.broadcasted_iota(jnp.int32, (TQ, 1), 0)
    qpos1 = past_len + tok1
    lane = lax.broadcasted_iota(jnp.int32, (TQ, LANES), 1)
    mask_c = (lane < N_CMP) & (lane * CMP_STRIDE + (CMP_BLOCK - 1) <= qpos1)
    newj = lax.broadcasted_iota(jnp.int32, (TQ, TQ), 1)
    newt = lax.broadcasted_iota(jnp.int32, (TQ, TQ), 0)
    mask_new = (newj < n_new) & (newj <= newt)
    wcache = wb_ref.shape[1]
    wi = lax.broadcasted_iota(jnp.int32, (TQ, wcache), 1)
    wpos = past_len - wcache + wi
    mask_w = (wpos <= qpos1) & (wpos > qpos1 - WINDOW)
    n_blk = -(-(past_len + n_new) // SEL_BLOCK)
    gt = jax.nn.sigmoid(gt_ref[0])

    def bcast(mask):
        return jnp.broadcast_to(mask[None], (HPG,) + mask.shape)

    def two_part_attention(s_a, mask_a, s_b, mask_b):
        s_a = jnp.where(mask_a, s_a, NEG)
        s_b = jnp.where(mask_b, s_b, NEG)
        m = jnp.maximum(jnp.max(s_a, -1, keepdims=True), jnp.max(s_b, -1, keepdims=True))
        e_a = jnp.where(mask_a, jnp.exp(s_a - m), 0.0)
        e_b = jnp.where(mask_b, jnp.exp(s_b - m), 0.0)
        den = jnp.maximum(jnp.sum(e_a, -1, keepdims=True) + jnp.sum(e_b, -1, keepdims=True), 1e-30)
        return e_a / den, e_b / den

    for g in range(KV_HEADS):
        gs = slice(g * HEAD_DIM, (g + 1) * HEAD_DIM)
        vs_cols = slice(KV_COLS + g * HEAD_DIM, KV_COLS + (g + 1) * HEAD_DIM)
        qg = jnp.concatenate(
            [q_ref[0, :, (g * HPG + h) * HEAD_DIM:(g * HPG + h + 1) * HEAD_DIM] for h in range(HPG)],
            axis=0) * scale
        qg = qg.astype(BF16)
        s = _bdot_nt(qg, kc_ref[0, :, gs]).reshape(HPG, TQ, N_SEG)
        p = _masked_softmax(s, mask_c[None])
        o_cmp = _bdot(p.reshape(rows, N_SEG), vc_ref[0, :, gs])
        imp = _dot_exact_rhs(p[0] + p[1] + p[2] + p[3], ov_ref[...])
        sel = _select_blocks(imp, qpos1 // SEL_BLOCK, n_blk)
        past_mask = jnp.dot(sel.astype(BF16), ex_ref[...], preferred_element_type=F32) > 0.5
        new_blk = past_len // SEL_BLOCK
        mask_sn = mask_new & (sel[:, new_blk:new_blk + 1] > 0.5)
        s_past = jnp.concatenate([_bdot_nt(qg, pg[0, :, gs]) for pg in pages], axis=1)
        s_new = _bdot_nt(qg, new_ref[0, :, 2 * KV_COLS + g * HEAD_DIM:2 * KV_COLS + (g + 1) * HEAD_DIM])
        p_past, p_new = two_part_attention(s_past.reshape(HPG, TQ, -1), bcast(past_mask),
                                           s_new.reshape(HPG, TQ, TQ), bcast(mask_sn))
        p_past = p_past.reshape(rows, -1)
        o_sel = _bdot(p_new.reshape(rows, TQ),
                      new_ref[0, :, 3 * KV_COLS + g * HEAD_DIM:3 * KV_COLS + (g + 1) * HEAD_DIM])
        for k, pg in enumerate(pages):
            o_sel = o_sel + _bdot(p_past[:, k * PAGE_SIZE:(k + 1) * PAGE_SIZE], pg[0, :, vs_cols])
        s_wb = _bdot_nt(qg, wb_ref[0, :, gs])
        s_wn = _bdot_nt(qg, wn_ref[0, :, gs])
        p_wb, p_wn = two_part_attention(s_wb.reshape(HPG, TQ, wcache), bcast(mask_w),
                                        s_wn.reshape(HPG, TQ, TQ), bcast(mask_new))
        o_win = (_bdot(p_wb.reshape(rows, wcache), wb_ref[0, :, vs_cols])
                 + _bdot(p_wn.reshape(rows, TQ), wn_ref[0, :, vs_cols]))
        for h in range(HPG):
            r8 = slice(h * TQ, (h + 1) * TQ)
            gc = (g // 2) * LANES + ((g % 2) * HPG + h) * 3
            o = (gt[:, gc:gc + 1] * o_cmp[r8] + gt[:, gc + 1:gc + 2] * o_sel[r8]
                 + gt[:, gc + 2:gc + 3] * o_win[r8])
            c0 = (g * HPG + h) * HEAD_DIM
            o_ref[0, :, c0:c0 + HEAD_DIM] = o


def _nsa_sample(pages, layer, page_ids, q, gates, kc, vc, new_rows, win_buf, win_new, past_len, n_new):
    n = q.shape[0]
    wc = win_buf.shape[1]
    n_blk = -(-(past_len + n_new) // SEL_BLOCK)
    page_spec = lambda k: pl.BlockSpec((1, PAGE_SIZE, 2 * KV_COLS),
                                       lambda b, pt: (pt[b * PAGES_PER_SEQ + k], 0, 2 * layer + 1))
    seq = lambda a: pl.BlockSpec((1,) + a.shape[1:], lambda b, pt: (b,) + (0,) * (a.ndim - 1))
    full = lambda a: pl.BlockSpec(a.shape, lambda b, pt: (0,) * a.ndim)
    ov, ex = _overlap_matrix(n_blk), _expand_matrix(past_len)
    per_seq = (q, gates, kc, vc, new_rows, win_buf, win_new)
    grid_spec = pltpu.PrefetchScalarGridSpec(
        num_scalar_prefetch=1,
        grid=(n,),
        in_specs=([page_spec(k) for k in range(PAGES_PER_SEQ)] + [seq(a) for a in per_seq]
                  + [full(ov), full(ex)]),
        out_specs=pl.BlockSpec((1, TQ, N_HEADS * HEAD_DIM), lambda b, pt: (b, 0, 0)),
    )
    return pl.pallas_call(
        functools.partial(_nsa_sample_kernel, past_len=past_len, n_new=n_new),
        grid_spec=grid_spec,
        out_shape=jax.ShapeDtypeStruct((n, TQ, N_HEADS * HEAD_DIM), F32),
        compiler_params=_cparams(("parallel",)),
        name="nsa_sample",
    )(page_ids, *([pages] * PAGES_PER_SEQ), *per_seq, ov, ex)


WKV_SAMPLE_CHUNK = 16
WKV_PROMPT_SEQS = 2
WKV_SAMPLE_SEQS = 8


def _pad_rows(a, n):
    return jnp.pad(a, ((0, 0), (0, n - a.shape[1]), (0, 0)))


def _nsa_layer(x, n_prompt, b, t, db, ds, pages, layer, page_table, win_cache, w_in, w_out, cmp, ln_g,
               ln_b):
    d = x.shape[1]
    qd = N_HEADS * HEAD_DIM
    q = _proj(x, w_in, 512, 0, qd)
    rows = _proj(x, w_in, 512, 2, 4 * KV_COLS)
    win = _proj(x, w_in, 512, 4, 2 * KV_COLS)
    n_gate = 2 * HPG * 3
    g0 = qd + 6 * KV_COLS
    wg = jnp.concatenate(
        [jnp.pad(w_in[:, g0 + p * n_gate:g0 + (p + 1) * n_gate], ((0, 0), (0, LANES - n_gate)))
         for p in range(KV_HEADS // 2)], axis=1)
    gates = _proj(x, wg, wg.shape[1], 0, wg.shape[1])
    past_len = page_table.shape[1] * PAGE_SIZE

    prompt_pages = rows.reshape(rows.shape[0] // PAGE_SIZE, PAGE_SIZE, 4 * KV_COLS)
    kc_p, vc_p = _compress(prompt_pages, 0, jnp.arange(b * PAGES_PER_SEQ, dtype=jnp.int32), b, *cmp)
    o_p = _nsa_prompt(q, gates, kc_p, vc_p, rows, win, b, t)

    q_s, rows_s, win_s, gates_s = (a[n_prompt:].reshape(db, ds, -1) for a in (q, rows, win, gates))
    page_ids = page_table.reshape(-1).astype(jnp.int32)
    kc_s, vc_s = _compress(pages, layer, page_ids, db, *cmp)
    win_buf = win_cache.reshape(db, win_cache.shape[1], 2 * KV_COLS)
    o_s = _nsa_sample(pages, layer, page_ids, _pad_rows(q_s, TQ), _pad_rows(gates_s, TQ), kc_s, vc_s,
                      _pad_rows(rows_s, TQ), win_buf, _pad_rows(win_s, TQ), past_len, ds)[:, :ds]

    o = jnp.concatenate([o_p, o_s.reshape(db * ds, qd)], axis=0)
    x_new = _out_ln(o, None, w_out, x, ln_g, ln_b)
    keep = min(WINDOW, t)
    win_p = win[:n_prompt].reshape(b, t, 2 * KV_COLS)
    outs = (rows[:n_prompt].reshape(b, t, 4, KV_HEADS, HEAD_DIM),
            win_p[:, t - keep:].reshape(b, keep, 2, KV_HEADS, HEAD_DIM),
            rows_s.reshape(db, ds, 4, KV_HEADS, HEAD_DIM),
            jnp.concatenate([win_buf, win_s], axis=1)[:, ds:].reshape(db, -1, 2, KV_HEADS, HEAD_DIM))
    return x_new, outs


def _rwkv_layer(x, n_prompt, b, t, db, ds, shift_s, state_s, mu, w_rkv, w0, w1, w2, a0, a1, a2, g1, g2,
                k_k, k_a, r_k, gn_g, gn_b, w_out, ln_g, ln_b):
    d = x.shape[1]
    xs = x[n_prompt:].reshape(db, ds, d)
    prev_s = jnp.concatenate([shift_s[:, None].astype(x.dtype), xs[:, :-1]], axis=1).reshape(db * ds, d)
    tiles, p_tiles = x.shape[0] // ROW_TILE, n_prompt // ROW_TILE
    last_rows = x[ROW_TILE - 1:n_prompt:ROW_TILE]
    carry = jnp.concatenate([jnp.zeros((1, d), x.dtype), last_rows[:-1]], axis=0)
    seq_start = (jnp.arange(p_tiles) * ROW_TILE) % t == 0
    carry = jnp.where(seq_start[:, None], 0.0, carry)
    carry = jnp.pad(carry, ((0, tiles - p_tiles), (0, 0))).reshape(tiles, 1, d)
    r, k, v, ld, ag, g = _rwkv_pre(x, carry, prev_s, mu, w_rkv, w0, w1, w2, a0, a1, a2, g1, g2)
    streams = (r, k, v, ld, ag)
    head_params = (k_k, k_a, r_k.reshape(d), gn_g, gn_b)
    zero_state = jnp.zeros((b, RWKV_HEADS, RWKV_HEAD, RWKV_HEAD), F32)
    y_p, wkv_p = _wkv(streams, b, t, zero_state, *head_params, WKV_CHUNK, WKV_PROMPT_SEQS)
    streams_s = [_pad_rows(a[n_prompt:].reshape(db, ds, d), WKV_SAMPLE_CHUNK).reshape(-1, d)
                 for a in streams]
    y_s, wkv_s = _wkv(streams_s, db, WKV_SAMPLE_CHUNK, state_s.astype(F32), *head_params,
                      WKV_SAMPLE_CHUNK, WKV_SAMPLE_SEQS)
    y = jnp.concatenate([y_p.reshape(n_prompt, d), y_s[:, :ds].reshape(db * ds, d)], axis=0)
    x_new = _out_ln(y, g, w_out, x, ln_g, ln_b)
    return x_new, (wkv_p, x[t - 1:n_prompt:t], wkv_s.astype(state_s.dtype), xs[:, -1])


def kernel(x_prompt, x_sample, cache_nsa_paged, cache_nsa_win, state_rwkv_wkv, state_rwkv_shift, page_table, nsa_w_in, nsa_w_out, nsa_cmp_pe, nsa_cmp_w1, nsa_cmp_b1, nsa_cmp_w2, nsa_cmp_b2, rwkv_mu, rwkv_w_rkv, rwkv_w0, rwkv_w1, rwkv_w2, rwkv_a0, rwkv_a1, rwkv_a2, rwkv_g1, rwkv_g2, rwkv_k_k, rwkv_k_a, rwkv_r_k, rwkv_gn_g, rwkv_gn_b, rwkv_w_out, ln_g, ln_b, moe_w_router, moe_b_router, moe_w_up, moe_b_up, moe_w_down, moe_b_down):
    b, t, d = x_prompt.shape
    db, ds, _ = x_sample.shape
    n_prompt = b * t
    x = jnp.concatenate([x_prompt.reshape(n_prompt, d), x_sample.reshape(db * ds, d)], axis=0)
    n_phys = cache_nsa_paged.shape[0]
    nsa_outs, rwkv_outs = [], []
    for i in range(DEPTH):
        j = i // 2
        if i % 2 == 0:
            pages = cache_nsa_paged.reshape(n_phys, PAGE_SIZE, -1)
            cmp = (nsa_cmp_pe[j], nsa_cmp_w1[j], nsa_cmp_b1[j], nsa_cmp_w2[j], nsa_cmp_b2[j])
            x, outs = _nsa_layer(x, n_prompt, b, t, db, ds, pages, j, page_table, cache_nsa_win[j],
                                 nsa_w_in[j], nsa_w_out[j], cmp, ln_g[i, 0], ln_b[i, 0])
            nsa_outs.append(outs)
        else:
            x, outs = _rwkv_layer(x, n_prompt, b, t, db, ds, state_rwkv_shift[j], state_rwkv_wkv[j],
                                  rwkv_mu[j], rwkv_w_rkv[j], rwkv_w0[j], rwkv_w1[j], rwkv_w2[j],
                                  rwkv_a0[j], rwkv_a1[j], rwkv_a2[j], rwkv_g1[j], rwkv_g2[j],
                                  rwkv_k_k[j], rwkv_k_a[j], rwkv_r_k[j], rwkv_gn_g[j], rwkv_gn_b[j],
                                  rwkv_w_out[j], ln_g[i, 0], ln_b[i, 0])
            rwkv_outs.append(outs)
        x = _moe_layer(x, moe_w_router[i], moe_b_router[i], moe_w_up[i], moe_b_up[i], moe_w_down[i],
                       moe_b_down[i], ln_g[i, 1], ln_b[i, 1])
    p_rows, p_win, s_rows, s_win = (jnp.stack(z, axis=k) for z, k in zip(zip(*nsa_outs), (2, 0, 2, 0)))
    p_wkv, p_shift, s_wkv, s_shift = (jnp.stack(z, axis=0) for z in zip(*rwkv_outs))
    return (x[:n_prompt].reshape(b, t, d), x[n_prompt:].reshape(db, ds, d),
            p_rows, p_win, p_wkv, p_shift, s_rows, s_win, s_wkv, s_shift)
```

```python
import functools

import numpy as np
import jax
import jax.numpy as jnp
from jax import lax
from jax.experimental import pallas as pl
from jax.experimental.pallas import tpu as pltpu

F32 = jnp.float32
BF16 = jnp.bfloat16

D_MODEL = 1024
DEPTH = 2
N_HEADS = 16
HEAD_DIM = 64
KV_HEADS = 4
HPG = N_HEADS // KV_HEADS
CMP_BLOCK = 32
CMP_STRIDE = 16
SEL_BLOCK = 64
N_SEL = 16
WINDOW = 512
PAGE_SIZE = 128
RWKV_HEAD = 64
RWKV_HEADS = D_MODEL // RWKV_HEAD
GN_EPS = 64e-5
N_EXPERTS = 32
TOP_K = 4
D_EXPERT = D_MODEL
SWIGLU_ALPHA = 1.702
SWIGLU_LIMIT = 7.0
LN_EPS = 1e-5
DN_ALPHA = (2 * DEPTH) ** 0.25

LANES = 128
QB = 128
KEY_STEP = 256
NEG = -1e30
VMEM_LIMIT = 56 * 1024 * 1024
ROW_TILE = 512
MOE_ROWS = 256
MOE_COL_TILE = 256
WKV_CHUNK = 64


def _cparams(sem):
    return pltpu.CompilerParams(dimension_semantics=sem, vmem_limit_bytes=VMEM_LIMIT)


def _bdot(a, b):
    return jnp.dot(a.astype(BF16), b.astype(BF16), preferred_element_type=F32)


def _bdot_nt(a, b):
    return lax.dot_general(a.astype(BF16), b.astype(BF16), (((1,), (1,)), ((), ())),
                           preferred_element_type=F32)


def _bdot_tn(a, b):
    return lax.dot_general(a.astype(BF16), b.astype(BF16), (((0,), (0,)), ((), ())),
                           preferred_element_type=F32)


def _bmm(a, b):
    return lax.dot_general(a.astype(BF16), b.astype(BF16), (((2,), (1,)), ((0,), (0,))),
                           preferred_element_type=F32)


def _bmm_nt(a, b):
    return lax.dot_general(a.astype(BF16), b.astype(BF16), (((2,), (2,)), ((0,), (0,))),
                           preferred_element_type=F32)


def _split3(x):
    hi = x.astype(BF16)
    r = x - hi.astype(F32)
    mid = r.astype(BF16)
    lo = (r - mid.astype(F32)).astype(BF16)
    return hi, mid, lo


def _dot_exact_rhs(x, w01):
    hi, mid, lo = _split3(x)
    d = lambda a: jnp.dot(a, w01, preferred_element_type=F32)
    return d(hi) + d(mid) + d(lo)


def _dot_exact_lhs(w01, x):
    hi, mid, lo = _split3(x)
    d = lambda a: jnp.dot(w01, a, preferred_element_type=F32)
    return d(hi) + d(mid) + d(lo)


def _layer_norm(z, g, b):
    mu = jnp.mean(z, -1, keepdims=True)
    zc = z - mu
    var = jnp.mean(zc * zc, -1, keepdims=True)
    return zc * lax.rsqrt(var + LN_EPS) * g + b


def _proj_kernel(x_ref, w_ref, o_ref):
    o_ref[...] = _bdot(x_ref[...], w_ref[...])


def _proj(x, w, tn, col_block0, n_out):
    m, k = x.shape
    return pl.pallas_call(
        _proj_kernel,
        grid=(n_out // tn, m // ROW_TILE),
        in_specs=[pl.BlockSpec((ROW_TILE, k), lambda j, i: (i, 0)),
                  pl.BlockSpec((k, tn), lambda j, i: (0, j + col_block0))],
        out_specs=pl.BlockSpec((ROW_TILE, tn), lambda j, i: (i, j)),
        out_shape=jax.ShapeDtypeStruct((m, n_out), F32),
        compiler_params=_cparams(("parallel", "parallel")),
        name="proj",
    )(x, w)


def _out_ln_kernel(*refs, gated):
    if gated:
        a_ref, m_ref, w_ref, res_ref, g_ref, b_ref, o_ref = refs
        a = a_ref[...] * m_ref[...]
    else:
        a_ref, w_ref, res_ref, g_ref, b_ref, o_ref = refs
        a = a_ref[...]
    z = DN_ALPHA * res_ref[...] + _bdot(a, w_ref[...])
    o_ref[...] = _layer_norm(z, g_ref[...], b_ref[...])


def _out_ln(a, mul, w, res, g, b):
    m, k = a.shape
    n = w.shape[1]
    row = lambda i: (i, 0)
    fixed = lambda i: (0, 0)
    ins = [a] + ([mul] if mul is not None else []) + [w, res, g.reshape(1, n), b.reshape(1, n)]
    specs = ([pl.BlockSpec((ROW_TILE, k), row)] * (2 if mul is not None else 1)
             + [pl.BlockSpec((k, n), fixed), pl.BlockSpec((ROW_TILE, n), row),
                pl.BlockSpec((1, n), fixed), pl.BlockSpec((1, n), fixed)])
    return pl.pallas_call(
        functools.partial(_out_ln_kernel, gated=mul is not None),
        grid=(m // ROW_TILE,),
        in_specs=specs,
        out_specs=pl.BlockSpec((ROW_TILE, n), row),
        out_shape=jax.ShapeDtypeStruct((m, n), F32),
        compiler_params=_cparams(("parallel",)),
        name="out_ln",
    )(*ins)


def _router_kernel(x_ref, w_ref, b_ref, e_ref, g_ref):
    xh, xm, xl = _split3(x_ref[...])
    wh, wm, wl = _split3(w_ref[...])
    d = lambda a, c: jnp.dot(a, c, preferred_element_type=F32)
    acc = d(xh, wh) + (d(xh, wm) + d(xm, wh)) + (d(xm, wm) + d(xh, wl) + d(xl, wh))
    lane = lax.broadcasted_iota(jnp.int32, acc.shape, 1).astype(F32)
    logits = jnp.where(lane < N_EXPERTS, acc + b_ref[...], -jnp.inf)
    vals, ids = [], []
    for _ in range(TOP_K):
        m = jnp.max(logits, -1, keepdims=True)
        idx = jnp.min(jnp.where(logits == m, lane, float(LANES)), -1, keepdims=True)
        vals.append(m)
        ids.append(idx)
        logits = jnp.where(lane == idx, -jnp.inf, logits)
    es = [jnp.exp(v - vals[0]) for v in vals]
    den = functools.reduce(lambda a, c: a + c, es)
    e_out = jnp.zeros(acc.shape, F32)
    g_out = jnp.zeros(acc.shape, F32)
    for k in range(TOP_K):
        e_out = jnp.where(lane == k, ids[k], e_out)
        g_out = jnp.where(lane == k, es[k] / den, g_out)
    e_ref[...] = e_out
    g_ref[...] = g_out


def _router(x, w_pad, b_pad):
    m, k = x.shape
    out = pl.BlockSpec((ROW_TILE, LANES), lambda i: (i, 0))
    return pl.pallas_call(
        _router_kernel,
        grid=(m // ROW_TILE,),
        in_specs=[pl.BlockSpec((ROW_TILE, k), lambda i: (i, 0)),
                  pl.BlockSpec((k, LANES), lambda i: (0, 0)),
                  pl.BlockSpec((1, LANES), lambda i: (0, 0))],
        out_specs=[out, out],
        out_shape=[jax.ShapeDtypeStruct((m, LANES), F32)] * 2,
        compiler_params=_cparams(("parallel",)),
        name="router",
    )(x, w_pad, b_pad)


def _moe_kernel(be_ref, nb_ref, x_ref, wu_ref, bu_ref, wd_ref, bd_ref, gw_ref, o_ref, wu_bf, wd_bf, h_bf):
    i = pl.program_id(0)
    used = i < nb_ref[0]

    @pl.when(used & ((i == 0) | (be_ref[i] != be_ref[jnp.maximum(i - 1, 0)])))
    def _():
        wu_bf[...] = wu_ref[0].astype(BF16)
        wd_bf[...] = wd_ref[0].astype(BF16)

    @pl.when(used)
    def _():
        xb = x_ref[...].astype(BF16)
        for j in range(D_EXPERT // MOE_COL_TILE):
            cg = slice(j * MOE_COL_TILE, (j + 1) * MOE_COL_TILE)
            cl = slice(D_EXPERT + j * MOE_COL_TILE, D_EXPERT + (j + 1) * MOE_COL_TILE)
            glu = jnp.dot(xb, wu_bf[:, cg], preferred_element_type=F32) + bu_ref[0, :, cg]
            lin = jnp.dot(xb, wu_bf[:, cl], preferred_element_type=F32) + bu_ref[0, :, cl]
            glu = jnp.minimum(glu, SWIGLU_LIMIT)
            lin = jnp.clip(lin, -SWIGLU_LIMIT, SWIGLU_LIMIT)
            h_bf[:, cg] = (glu * jax.nn.sigmoid(SWIGLU_ALPHA * glu) * (lin + 1.0)).astype(BF16)
        y = jnp.dot(h_bf[...], wd_bf[...], preferred_element_type=F32) + bd_ref[0]
        o_ref[...] = y * gw_ref[...]

    @pl.when(i >= nb_ref[0])
    def _():
        o_ref[...] = jnp.zeros_like(o_ref)


def _moe_experts(xg, gw, block_e, n_used, w_up, b_up, w_down, b_down):
    r, d = xg.shape
    nblk = r // MOE_ROWS
    e, _, d2 = w_up.shape
    grid_spec = pltpu.PrefetchScalarGridSpec(
        num_scalar_prefetch=2,
        grid=(nblk,),
        in_specs=[pl.BlockSpec((MOE_ROWS, d), lambda i, be, nb: (i, 0)),
                  pl.BlockSpec((1, d, d2), lambda i, be, nb: (be[i], 0, 0)),
                  pl.BlockSpec((1, 1, d2), lambda i, be, nb: (be[i], 0, 0)),
                  pl.BlockSpec((1, d2 // 2, d), lambda i, be, nb: (be[i], 0, 0)),
                  pl.BlockSpec((1, 1, d), lambda i, be, nb: (be[i], 0, 0)),
                  pl.BlockSpec((MOE_ROWS, 1), lambda i, be, nb: (i, 0))],
        out_specs=pl.BlockSpec((MOE_ROWS, d), lambda i, be, nb: (i, 0)),
        scratch_shapes=[pltpu.VMEM((d, d2), BF16), pltpu.VMEM((d2 // 2, d), BF16),
                        pltpu.VMEM((MOE_ROWS, d2 // 2), BF16)],
    )
    return pl.pallas_call(
        _moe_kernel,
        grid_spec=grid_spec,
        out_shape=jax.ShapeDtypeStruct((r, d), F32),
        compiler_params=_cparams(("arbitrary",)),
        name="moe_experts",
    )(block_e, n_used, xg, w_up, b_up.reshape(e, 1, d2), w_down, b_down.reshape(e, 1, d), gw)


def _combine_ln_kernel(res_ref, y_ref, g_ref, b_ref, o_ref):
    y = y_ref[0] + y_ref[1] + y_ref[2] + y_ref[3]
    o_ref[...] = _layer_norm(DN_ALPHA * res_ref[...] + y, g_ref[...], b_ref[...])


def _combine_ln(res, y4, g, b):
    m, n = res.shape
    return pl.pallas_call(
        _combine_ln_kernel,
        grid=(m // ROW_TILE,),
        in_specs=[pl.BlockSpec((ROW_TILE, n), lambda i: (i, 0)),
                  pl.BlockSpec((TOP_K, ROW_TILE, n), lambda i: (0, i, 0)),
                  pl.BlockSpec((1, n), lambda i: (0, 0)),
                  pl.BlockSpec((1, n), lambda i: (0, 0))],
        out_specs=pl.BlockSpec((ROW_TILE, n), lambda i: (i, 0)),
        out_shape=jax.ShapeDtypeStruct((m, n), F32),
        compiler_params=_cparams(("parallel",)),
        name="combine_ln",
    )(res, y4, g.reshape(1, n), b.reshape(1, n))


def _moe_layer(x, w_router, b_router, w_up, b_up, w_down, b_down, ln_g, ln_b):
    n = x.shape[0]
    wr = jnp.pad(w_router, ((0, 0), (0, LANES - N_EXPERTS)))
    br = jnp.pad(b_router, (0, LANES - N_EXPERTS)).reshape(1, LANES)
    ids, gates = _router(x, wr, br)
    top_e = ids[:, :TOP_K].astype(jnp.int32)
    gate = gates[:, :TOP_K]
    n_assign = n * TOP_K
    flat_e = top_e.reshape(-1).astype(jnp.int32)
    iota = jnp.arange(n_assign, dtype=jnp.int32)
    e_sorted, order = lax.sort((flat_e, iota), num_keys=1)
    _, sorted_pos = lax.sort((order, iota), num_keys=1)
    experts = jnp.arange(N_EXPERTS, dtype=jnp.int32)
    counts = jnp.sum((flat_e[None, :] == experts[:, None]).astype(jnp.int32), axis=1)
    raw_start = jnp.cumsum(counts) - counts
    padded = (counts + MOE_ROWS - 1) // MOE_ROWS * MOE_ROWS
    pad_end = jnp.cumsum(padded)
    pad_start = pad_end - padded
    n_blocks = -(-n_assign // MOE_ROWS) + N_EXPERTS
    n_rows = n_blocks * MOE_ROWS
    block_start = jnp.arange(n_blocks, dtype=jnp.int32) * MOE_ROWS
    block_e = jnp.minimum(jnp.sum(pad_end[None, :] <= block_start[:, None], axis=1),
                          N_EXPERTS - 1).astype(jnp.int32)
    n_used = (pad_end[-1:] // MOE_ROWS).astype(jnp.int32)
    j = (block_start - pad_start[block_e])[:, None] + jnp.arange(MOE_ROWS, dtype=jnp.int32)[None, :]
    valid = (j < counts[block_e][:, None]).reshape(-1)
    src = jnp.clip(raw_start[block_e][:, None] + j, 0, n_assign - 1).reshape(-1)
    assign_of_row = order[src]
    tok_of_row = jnp.where(valid, assign_of_row // TOP_K, 0)
    gate_of_row = jnp.where(valid, gate.reshape(-1)[assign_of_row], 0.0)
    row_of_assign = (pad_start - raw_start)[flat_e] + sorted_pos
    xg = x[tok_of_row]
    yb = _moe_experts(xg, gate_of_row.reshape(n_rows, 1), block_e, n_used, w_up, b_up, w_down, b_down)
    y4 = yb[row_of_assign.reshape(n, TOP_K).T]
    return _combine_ln(x, y4, ln_g, ln_b)


def _rwkv_pre_kernel(x_ref, carry_ref, xps_ref, mu_ref, wrkv_ref, w0_ref, w1_ref, w2_ref, a0_ref, a1_ref,
                     a2_ref, g1_ref, g2_ref, r_ref, k_ref, v_ref, ld_ref, ag_ref, g_ref, *, prompt_tiles):
    x = x_ref[...]
    first = lax.broadcasted_iota(jnp.int32, x.shape, 0) == 0
    x_prev = jnp.where(first, carry_ref[0], pltpu.roll(x, 1, 0))
    x_prev = jnp.where(pl.program_id(0) >= prompt_tiles, xps_ref[...], x_prev)
    dx = x_prev - x
    mix = lambda s: x + dx * mu_ref[s:s + 1, :]
    r_ref[...] = _bdot(mix(0), wrkv_ref[0])
    k_ref[...] = _bdot(mix(1), wrkv_ref[1])
    v_ref[...] = _bdot(mix(2), wrkv_ref[2])
    lw = w0_ref[...] + _bdot(jnp.tanh(_bdot(mix(3), w1_ref[...])), w2_ref[...])
    z = -lw
    softplus = jnp.maximum(z, 0.0) + jnp.log1p(jnp.exp(-jnp.abs(z)))
    ld_ref[...] = -jnp.exp(-softplus - 0.5)
    ag_ref[...] = jax.nn.sigmoid(a0_ref[...] + _bdot(_bdot(mix(4), a1_ref[...]), a2_ref[...]))
    g_ref[...] = _bdot(jax.nn.sigmoid(_bdot(mix(5), g1_ref[...])), g2_ref[...])


def _rwkv_pre(x, carry, x_prev_sample, mu, w_rkv, w0, w1, w2, a0, a1, a2, g1, g2):
    m, d = x.shape
    prompt_tiles = (m - x_prev_sample.shape[0]) // ROW_TILE
    row = pl.BlockSpec((ROW_TILE, d), lambda i: (i, 0))
    full = lambda a: pl.BlockSpec(a.shape, lambda i: (0,) * a.ndim)
    w0, a0 = w0.reshape(1, d), a0.reshape(1, d)
    weights = (mu, w_rkv, w0, w1, w2, a0, a1, a2, g1, g2)
    return pl.pallas_call(
        functools.partial(_rwkv_pre_kernel, prompt_tiles=prompt_tiles),
        grid=(m // ROW_TILE,),
        in_specs=[row, pl.BlockSpec((1, 1, d), lambda i: (i, 0, 0)),
                  pl.BlockSpec((ROW_TILE, d), lambda i: (jnp.maximum(i - prompt_tiles, 0), 0))]
                 + [full(a) for a in weights],
        out_specs=[row] * 6,
        out_shape=[jax.ShapeDtypeStruct((m, d), F32)] * 6,
        compiler_params=_cparams(("parallel",)),
        name="rwkv_pre",
    )(x, carry, x_prev_sample, *weights)


def _wkv_kernel(*refs, chunk, nseq):
    streams = [refs[j * nseq:(j + 1) * nseq] for j in range(5)]
    kk_ref, ka_ref, rk_ref, gg_ref, gb_ref, s0_ref, y_ref, sout_ref, s_ref = refs[5 * nseq:]
    c = pl.program_id(1)
    nh = nseq * RWKV_HEADS

    @pl.when(c == 0)
    def _():
        s_ref[...] = s0_ref[...].reshape(nh, RWKV_HEAD, RWKV_HEAD)

    row = lax.broadcasted_iota(jnp.int32, (chunk, chunk), 0)
    col = lax.broadcasted_iota(jnp.int32, (chunk, chunk), 1)
    incl = (col <= row)[None]
    strict = (col < row)[None]
    tri = jnp.where(col <= row, 1.0, 0.0).astype(BF16)
    eye = jnp.where(col == row, 1.0, 0.0).astype(F32)[None]

    def heads(rows_of_seq):
        return jnp.stack([rows_of_seq(i)[:, h * RWKV_HEAD:(h + 1) * RWKV_HEAD]
                          for i in range(nseq) for h in range(RWKV_HEADS)], axis=0)

    def head_param(ref):
        return jnp.stack([ref[:, h * RWKV_HEAD:(h + 1) * RWKV_HEAD]
                          for _ in range(nseq) for h in range(RWKV_HEADS)], axis=0)

    r, k, v, ld, ag = (heads(lambda i, rs=rs: rs[i][...]) for rs in streams)
    lcum = [_dot_exact_lhs(tri, streams[3][i][...]) for i in range(nseq)]
    lc = heads(lambda i: lcum[i])
    kk = k * head_param(kk_ref)
    kk = kk / jnp.maximum(jnp.sqrt(jnp.sum(kk * kk, -1, keepdims=True)), 1e-12)
    kh = k * (1.0 + (ag - 1.0) * head_param(ka_ref))
    b = kk * ag
    lend = lc[:, chunk - 1:chunk, :]
    e_neg = jnp.exp(-lc)
    a_t = -kk * jnp.exp(lc - ld)
    r_t = r * jnp.exp(lc)
    b_t = b * e_neg
    k_t = kh * e_neg
    l_ab = jnp.where(strict, _bmm_nt(a_t, b_t), 0.0)
    l_ak = jnp.where(strict, _bmm_nt(a_t, k_t), 0.0)
    t_rb = jnp.where(incl, _bmm_nt(r_t, b_t), 0.0)
    t_rk = jnp.where(incl, _bmm_nt(r_t, k_t), 0.0)
    s_old = s_ref[...]
    rhs = _bmm_nt(a_t, s_old) + _bmm(l_ak, v)
    inv = eye + l_ab
    lp = l_ab
    n = 2
    while n < chunk:
        lp = _bmm(lp, lp)
        inv = inv + _bmm(inv, lp)
        n *= 2
    u = _bmm(inv, rhs)
    y = _bmm_nt(r_t, s_old) + _bmm(t_rb, u) + _bmm(t_rk, v)
    e_end = jnp.exp(lend - lc)
    uv_t = jnp.swapaxes(jnp.concatenate([u, v], axis=1), 1, 2)
    bke = jnp.concatenate([b * e_end, kh * e_end], axis=1)
    s_ref[...] = s_old * jnp.exp(lend) + _bmm(uv_t, bke)
    mu_y = jnp.mean(y, -1, keepdims=True)
    yc = y - mu_y
    var_y = jnp.mean(yc * yc, -1, keepdims=True)
    yn = yc * lax.rsqrt(var_y + GN_EPS) * head_param(gg_ref) + head_param(gb_ref)
    out = yn + jnp.sum(r * kh * head_param(rk_ref), -1, keepdims=True) * v
    for i in range(nseq):
        for h in range(RWKV_HEADS):
            y_ref[i, :, h * RWKV_HEAD:(h + 1) * RWKV_HEAD] = out[i * RWKV_HEADS + h]

    @pl.when(c == pl.num_programs(1) - 1)
    def _():
        sout_ref[...] = s_ref[...].reshape(sout_ref.shape)


def _wkv(streams, n, t, s0, k_k, k_a, r_k, gn_g, gn_b, chunk, nseq):
    d = streams[0].shape[1]
    nc = t // chunk
    seq_in = lambda s: pl.BlockSpec((chunk, d), lambda i, c: ((i * nseq + s) * nc + c, 0))
    seq = pl.BlockSpec((nseq, chunk, d), lambda i, c: (i, c, 0))
    par = pl.BlockSpec((1, d), lambda i, c: (0, 0))
    st = pl.BlockSpec((nseq, RWKV_HEADS, RWKV_HEAD, RWKV_HEAD), lambda i, c: (i, 0, 0, 0))
    vec = lambda a: a.reshape(1, d)
    r, k, v, ld, ag = ([a] * nseq for a in streams)
    return pl.pallas_call(
        functools.partial(_wkv_kernel, chunk=chunk, nseq=nseq),
        grid=(n // nseq, nc),
        in_specs=[seq_in(s) for _ in range(5) for s in range(nseq)] + [par] * 5 + [st],
        out_specs=[seq, st],
        out_shape=[jax.ShapeDtypeStruct((n, t, d), F32),
                   jax.ShapeDtypeStruct((n, RWKV_HEADS, RWKV_HEAD, RWKV_HEAD), F32)],
        scratch_shapes=[pltpu.VMEM((nseq * RWKV_HEADS, RWKV_HEAD, RWKV_HEAD), F32)],
        compiler_params=_cparams(("parallel", "arbitrary")),
        name="wkv",
    )(*r, *k, *v, *ld, *ag, vec(k_k), vec(k_a), vec(r_k), vec(gn_g), vec(gn_b), s0)


PAGES_PER_SEQ = 16
SEG_PER_PAGE = PAGE_SIZE // CMP_STRIDE
N_SEG = PAGES_PER_SEQ * SEG_PER_PAGE
N_CMP = N_SEG - 1
KV_COLS = KV_HEADS * HEAD_DIM


def _gelu_tanh(x):
    return x * (0.5 * (1.0 + jnp.tanh(np.sqrt(2.0 / np.pi) * (x + 0.044715 * (x * x * x)))))


def _compress_kernel(pt_ref, *refs):
    n_slab = 2 * KV_COLS // LANES
    pages = refs[:PAGES_PER_SEQ * n_slab]
    w1c_ref, w1_ref, pe_ref, b1_ref, w2_ref, b2_ref, kc_ref, vc_ref = refs[PAGES_PER_SEQ * n_slab:]
    valid = lax.broadcasted_iota(jnp.int32, (N_SEG, HEAD_DIM), 0) < N_CMP
    for typ, out_ref in ((0, kc_ref), (1, vc_ref)):
        pe8 = jnp.broadcast_to(pe_ref[typ], (8, CMP_BLOCK * HEAD_DIM))
        c0 = _bdot(pe8, w1_ref[typ])[0:1, :] + b1_ref[typ]
        for pair in range(KV_HEADS // 2):
            cb = typ * (KV_HEADS // 2) + pair
            acc = jnp.zeros((N_SEG, 4 * HEAD_DIM), F32)
            rows_at = lambda p: jnp.concatenate(
                [pages[k * n_slab + cb][0, pl.ds(p, SEG_PER_PAGE, stride=CMP_STRIDE), :]
                 for k in range(PAGES_PER_SEQ)], axis=0)
            for p in range(0, CMP_STRIDE, 2):
                xp = jnp.concatenate([rows_at(p), rows_at(p + 1)], axis=1)
                acc = acc + _bdot(xp, w1c_ref[typ, p // 2])
            for gg in range(2):
                g = pair * 2 + gg
                first = acc[:, gg * LANES:gg * LANES + HEAD_DIM]
                second = acc[:, gg * LANES + HEAD_DIM:(gg + 1) * LANES]
                second_next = pltpu.roll(second, N_SEG - 1, 0)
                hid = _gelu_tanh(first + second_next + c0)
                out = _bdot(hid, w2_ref[typ]) + b2_ref[typ]
                out_ref[0, :, g * HEAD_DIM:(g + 1) * HEAD_DIM] = jnp.where(valid, out, 0.0)


def _compress(pages, layer, page_ids, n_seq, pe, w1, b1, w2, b2):
    w1c = w1.reshape(2, 2, CMP_STRIDE, HEAD_DIM, HEAD_DIM).transpose(0, 2, 3, 1, 4).reshape(
        2, CMP_STRIDE, HEAD_DIM, 2 * HEAD_DIM)
    zero = jnp.zeros_like(w1c)
    w1c = jnp.concatenate([jnp.concatenate([w1c, zero], axis=3), jnp.concatenate([zero, w1c], axis=3)],
                          axis=2)
    w1c = w1c.reshape(2, CMP_STRIDE // 2, 2 * LANES, 4 * HEAD_DIM)
    n_slab = 2 * KV_COLS // LANES
    slab0 = layer * (4 * KV_COLS // LANES)
    page_spec = lambda k, cb: pl.BlockSpec((1, PAGE_SIZE, LANES),
                                           lambda b, pt: (pt[b * PAGES_PER_SEQ + k], 0, slab0 + cb))
    full = lambda a: pl.BlockSpec(a.shape, lambda b, pt: (0,) * a.ndim)
    consts = (w1c, w1, pe.reshape(2, 1, CMP_BLOCK * HEAD_DIM), b1.reshape(2, 1, HEAD_DIM), w2,
              b2.reshape(2, 1, HEAD_DIM))
    out_spec = pl.BlockSpec((1, N_SEG, KV_COLS), lambda b, pt: (b, 0, 0))
    grid_spec = pltpu.PrefetchScalarGridSpec(
        num_scalar_prefetch=1,
        grid=(n_seq,),
        in_specs=([page_spec(k, cb) for k in range(PAGES_PER_SEQ) for cb in range(n_slab)]
                  + [full(a) for a in consts]),
        out_specs=[out_spec, out_spec],
    )
    return pl.pallas_call(
        _compress_kernel,
        grid_spec=grid_spec,
        out_shape=[jax.ShapeDtypeStruct((n_seq, N_SEG, KV_COLS), F32)] * 2,
        compiler_params=_cparams(("parallel",)),
        name="nsa_compress",
    )(page_ids, *([pages] * (PAGES_PER_SEQ * n_slab)), *consts)


def _overlap_matrix(n_blk):
    c0 = np.arange(N_CMP)[:, None] * CMP_STRIDE
    c1 = c0 + CMP_BLOCK - 1
    s0 = np.arange(n_blk)[None, :] * SEL_BLOCK
    s1 = s0 + SEL_BLOCK - 1
    ov = np.zeros((N_SEG, LANES), np.float32)
    ov[:N_CMP, :n_blk] = (c0 <= s1) & (c1 >= s0)
    return jnp.asarray(ov, BF16)


def _expand_matrix(n_keys):
    ex = (np.arange(LANES)[:, None] == (np.arange(n_keys)[None, :] // SEL_BLOCK)).astype(np.float32)
    return jnp.asarray(ex, BF16)


def _masked_softmax(s, mask):
    s = jnp.where(mask, s, NEG)
    m = jnp.max(s, -1, keepdims=True)
    e = jnp.where(mask, jnp.exp(s - m), 0.0)
    return e / jnp.maximum(jnp.sum(e, -1, keepdims=True), 1e-30)


def _select_blocks(imp, qblk, n_blk):
    lane = lax.broadcasted_iota(jnp.int32, imp.shape, 1)
    forced = (lane == 0) | (lane == qblk) | (lane == qblk - 1)
    imp = jnp.where(forced, jnp.inf, imp)
    imp = jnp.where(lane <= qblk, imp, -jnp.inf)
    rank = jnp.zeros(imp.shape, F32)
    for s in range(n_blk):
        c = imp[:, s:s + 1]
        tie_ahead = jnp.where(lane > s, 1.0, 0.0)
        rank = rank + jnp.where(c > imp, 1.0, jnp.where(c == imp, tie_ahead, 0.0))
    return jnp.where(rank < N_SEL, jnp.where(imp > -jnp.inf, 1.0, 0.0), 0.0)


def _select_blocks_t(imp, qblk, n_blk):
    blk = lax.broadcasted_iota(jnp.int32, imp.shape, 0)
    forced = (blk == 0) | (blk == qblk) | (blk == qblk - 1)
    imp = jnp.where(forced, jnp.inf, imp)
    imp = jnp.where(blk <= qblk, imp, -jnp.inf)
    rank = jnp.zeros(imp.shape, F32)
    for s in range(n_blk):
        c = imp[s:s + 1, :]
        tie_ahead = jnp.where(blk > s, 1.0, 0.0)
        rank = rank + jnp.where(c > imp, 1.0, jnp.where(c == imp, tie_ahead, 0.0))
    return jnp.where(rank < N_SEL, jnp.where(imp > -jnp.inf, 1.0, 0.0), 0.0)


def _softmax_rows_masked(s, mask):
    s = jnp.where(mask, s, NEG)
    m = jnp.max(s, 0, keepdims=True)
    e = jnp.where(mask, jnp.exp(s - m), 0.0)
    return e / jnp.maximum(jnp.sum(e, 0, keepdims=True), 1e-30)


def _nsa_prompt_kernel(q_ref, gt_ref, kc_ref, vc_ref, ks_ref, vs_ref, kw_ref, vw_ref, ovt_ref,
                       o_ref, qt_ref, mx_ref, m_ref, l_ref, acc_ref, oc_ref, *, n_chunks):
    qi = pl.program_id(2)
    scale = HEAD_DIM ** -0.5
    krow = lax.broadcasted_iota(jnp.int32, (QB, QB), 0)
    qcol = lax.broadcasted_iota(jnp.int32, (QB, QB), 1)
    qpos = qi * QB + qcol
    heads4 = lambda a: jnp.concatenate([a] * HPG, axis=1)

    m_ref[...] = jnp.full(m_ref.shape, NEG, F32)
    l_ref[...] = jnp.zeros(l_ref.shape, F32)
    acc_ref[...] = jnp.zeros(acc_ref.shape, F32)

    q_t = q_ref[...].T
    gt_t = jax.nn.sigmoid(gt_ref[...]).T
    cmp_seen = (krow < N_CMP) & (krow * CMP_STRIDE + (CMP_BLOCK - 1) <= qpos)
    mask_c = heads4(jnp.where(cmp_seen, 1.0, 0.0)) > 0.5
    qblk = (qi * QB + lax.broadcasted_iota(jnp.int32, (1, QB), 1)) // SEL_BLOCK
    n_blk = n_chunks * QB // SEL_BLOCK

    for g in range(2):
        for h in range(HPG):
            r0 = (g * HPG + h) * HEAD_DIM
            qt_ref[g, :, h * QB:(h + 1) * QB] = (q_t[r0:r0 + HEAD_DIM, :] * scale).astype(BF16)
        gs = slice(g * HEAD_DIM, (g + 1) * HEAD_DIM)
        p = _softmax_rows_masked(_bdot(kc_ref[0, :, gs], qt_ref[g]), mask_c)
        oc_ref[g] = _bdot_tn(vc_ref[0, :, gs], p)
        p_heads = p[:, 0:QB] + p[:, QB:2 * QB] + p[:, 2 * QB:3 * QB] + p[:, 3 * QB:4 * QB]
        imp = _dot_exact_lhs(ovt_ref[...], p_heads)
        sel = _select_blocks_t(imp[0:n_blk], qblk, n_blk)
        per_step = KEY_STEP // SEL_BLOCK
        for c in range(n_chunks * QB // KEY_STEP):
            mx_ref[g, c] = jnp.concatenate(
                [jnp.broadcast_to(sel[per_step * c + j:per_step * c + j + 1, :], (SEL_BLOCK, QB))
                 for j in range(per_step)], axis=0)

    def online_update(br, g, k, v, mask01):
        valid = heads4(mask01) > 0.5
        s = jnp.where(valid, _bdot(k, qt_ref[g]), NEG)
        m_old = m_ref[br, g]
        m_new = jnp.maximum(m_old, jnp.max(s, 0, keepdims=True))
        alpha = jnp.exp(m_old - m_new)
        p = jnp.where(valid, jnp.exp(s - m_new), 0.0)
        l_ref[br, g] = alpha * l_ref[br, g] + jnp.sum(p, 0, keepdims=True)
        acc_ref[br, g] = alpha * acc_ref[br, g] + _bdot_tn(v, p)
        m_ref[br, g] = m_new

    kstep = lax.broadcasted_iota(jnp.int32, (KEY_STEP, QB), 0)
    qstep = qi * QB + lax.broadcasted_iota(jnp.int32, (KEY_STEP, QB), 1)
    q_end = (qi + 1) * QB

    def sel_body(kc, carry):
        off = pl.multiple_of(kc * KEY_STEP, KEY_STEP)
        causal = off + kstep <= qstep
        for g in range(2):
            k = ks_ref[pl.ds(off, KEY_STEP), pl.ds(g * HEAD_DIM, HEAD_DIM)]
            v = vs_ref[pl.ds(off, KEY_STEP), pl.ds(g * HEAD_DIM, HEAD_DIM)]
            online_update(0, g, k, v, jnp.where(causal, mx_ref[g, kc], 0.0))
        return carry

    n_steps = (q_end + KEY_STEP - 1) // KEY_STEP
    lax.fori_loop(0, n_steps, sel_body, 0)

    def win_body(kc, carry):
        off = pl.multiple_of(kc * KEY_STEP, KEY_STEP)
        dist = qstep - (off + kstep)
        band = jnp.where((dist >= 0) & (dist < WINDOW), 1.0, 0.0)
        for g in range(2):
            k = kw_ref[pl.ds(off, KEY_STEP), pl.ds(g * HEAD_DIM, HEAD_DIM)]
            v = vw_ref[pl.ds(off, KEY_STEP), pl.ds(g * HEAD_DIM, HEAD_DIM)]
            online_update(1, g, k, v, band)
        return carry

    lax.fori_loop(jnp.maximum(qi * QB - WINDOW, 0) // KEY_STEP, n_steps, win_body, 0)

    pieces = []
    for g in range(2):
        o_sel = acc_ref[0, g] / jnp.maximum(l_ref[0, g], 1e-30)
        o_win = acc_ref[1, g] / jnp.maximum(l_ref[1, g], 1e-30)
        o_cmp = oc_ref[g]
        for h in range(HPG):
            cols = slice(h * QB, (h + 1) * QB)
            gc = (g * HPG + h) * 3
            pieces.append(gt_t[gc:gc + 1, :] * o_cmp[:, cols] + gt_t[gc + 1:gc + 2, :] * o_sel[:, cols]
                          + gt_t[gc + 2:gc + 3, :] * o_win[:, cols])
    o_ref[...] = jnp.concatenate(pieces, axis=0).T


def _nsa_prompt(q, gates, kc, vc, rows, win, b, t):
    nq = t // QB
    pair = 2 * HEAD_DIM
    blk = lambda shape, fn: pl.BlockSpec(shape, fn)
    seq_cols = lambda cb: blk((t, pair), lambda i, gp, qi: (i, cb + gp))
    return pl.pallas_call(
        functools.partial(_nsa_prompt_kernel, n_chunks=nq),
        grid=(b, KV_HEADS // 2, nq),
        in_specs=[blk((QB, 2 * HPG * HEAD_DIM), lambda i, gp, qi: (i * nq + qi, gp)),
                  blk((QB, LANES), lambda i, gp, qi: (i * nq + qi, gp)),
                  blk((1, N_SEG, pair), lambda i, gp, qi: (i, 0, gp)),
                  blk((1, N_SEG, pair), lambda i, gp, qi: (i, 0, gp)),
                  seq_cols(4), seq_cols(6),
                  seq_cols(0), seq_cols(2),
                  blk((LANES, N_SEG), lambda i, gp, qi: (0, 0))],
        out_specs=blk((QB, 2 * HPG * HEAD_DIM), lambda i, gp, qi: (i * nq + qi, gp)),
        out_shape=jax.ShapeDtypeStruct((b * t, N_HEADS * HEAD_DIM), F32),
        scratch_shapes=[pltpu.VMEM((2, HEAD_DIM, HPG * QB), BF16),
                        pltpu.VMEM((2, t // KEY_STEP, KEY_STEP, QB), F32),
                        pltpu.VMEM((2, 2, 1, HPG * QB), F32),
                        pltpu.VMEM((2, 2, 1, HPG * QB), F32),
                        pltpu.VMEM((2, 2, HEAD_DIM, HPG * QB), F32),
                        pltpu.VMEM((2, HEAD_DIM, HPG * QB), F32)],
        compiler_params=_cparams(("parallel", "parallel", "arbitrary")),
        name="nsa_prompt",
    )(q, gates, kc, vc, rows, rows, win, win, _overlap_matrix(t // SEL_BLOCK).T)


TQ = 8


def _nsa_sample_kernel(pt_ref, *refs, past_len, n_new):
    pages = refs[:PAGES_PER_SEQ]
    q_ref, gt_ref, kc_ref, vc_ref, new_ref, wb_ref, wn_ref, ov_ref, ex_ref, o_ref = refs[PAGES_PER_SEQ:]
    scale = HEAD_DIM ** -0.5
    rows = HPG * TQ
    tok1 = lax.broadcasted_iota(jnp.int32, (TQ, 1), 0)
    qpos1 = past_len + tok1
    lane = lax.broadcasted_iota(jnp.int32, (TQ, LANES), 1)
    mask_c = (lane < N_CMP) & (lane * CMP_STRIDE + (CMP_BLOCK - 1) <= qpos1)
    newj = lax.broadcasted_iota(jnp.int32, (TQ, TQ), 1)
    newt = lax.broadcasted_iota(jnp.int32, (TQ, TQ), 0)
    mask_new = (newj < n_new) & (newj <= newt)
    wcache = wb_ref.shape[1]
    wi = lax.broadcasted_iota(jnp.int32, (TQ, wcache), 1)
    wpos = past_len - wcache + wi
    mask_w = (wpos <= qpos1) & (wpos > qpos1 - WINDOW)
    n_blk = -(-(past_len + n_new) // SEL_BLOCK)
    gt = jax.nn.sigmoid(gt_ref[0])

    def bcast(mask):
        return jnp.broadcast_to(mask[None], (HPG,) + mask.shape)

    def two_part_attention(s_a, mask_a, s_b, mask_b):
        s_a = jnp.where(mask_a, s_a, NEG)
        s_b = jnp.where(mask_b, s_b, NEG)
        m = jnp.maximum(jnp.max(s_a, -1, keepdims=True), jnp.max(s_b, -1, keepdims=True))
        e_a = jnp.where(mask_a, jnp.exp(s_a - m), 0.0)
        e_b = jnp.where(mask_b, jnp.exp(s_b - m), 0.0)
        den = jnp.maximum(jnp.sum(e_a, -1, keepdims=True) + jnp.sum(e_b, -1, keepdims=True), 1e-30)
        return e_a / den, e_b / den

    for g in range(KV_HEADS):
        gs = slice(g * HEAD_DIM, (g + 1) * HEAD_DIM)
        vs_cols = slice(KV_COLS + g * HEAD_DIM, KV_COLS + (g + 1) * HEAD_DIM)
        qg = jnp.concatenate(
            [q_ref[0, :, (g * HPG + h) * HEAD_DIM:(g * HPG + h + 1) * HEAD_DIM] for h in range(HPG)],
            axis=0) * scale
        qg = qg.astype(BF16)
        s = _bdot_nt(qg, kc_ref[0, :, gs]).reshape(HPG, TQ, N_SEG)
        p = _masked_softmax(s, mask_c[None])
        o_cmp = _bdot(p.reshape(rows, N_SEG), vc_ref[0, :, gs])
        imp = _dot_exact_rhs(p[0] + p[1] + p[2] + p[3], ov_ref[...])
        sel = _select_blocks(imp, qpos1 // SEL_BLOCK, n_blk)
        past_mask = jnp.dot(sel.astype(BF16), ex_ref[...], preferred_element_type=F32) > 0.5
        new_blk = past_len // SEL_BLOCK
        mask_sn = mask_new & (sel[:, new_blk:new_blk + 1] > 0.5)
        s_past = jnp.concatenate([_bdot_nt(qg, pg[0, :, gs]) for pg in pages], axis=1)
        s_new = _bdot_nt(qg, new_ref[0, :, 2 * KV_COLS + g * HEAD_DIM:2 * KV_COLS + (g + 1) * HEAD_DIM])
        p_past, p_new = two_part_attention(s_past.reshape(HPG, TQ, -1), bcast(past_mask),
                                           s_new.reshape(HPG, TQ, TQ), bcast(mask_sn))
        p_past = p_past.reshape(rows, -1)
        o_sel = _bdot(p_new.reshape(rows, TQ),
                      new_ref[0, :, 3 * KV_COLS + g * HEAD_DIM:3 * KV_COLS + (g + 1) * HEAD_DIM])
        for k, pg in enumerate(pages):
            o_sel = o_sel + _bdot(p_past[:, k * PAGE_SIZE:(k + 1) * PAGE_SIZE], pg[0, :, vs_cols])
        s_wb = _bdot_nt(qg, wb_ref[0, :, gs])
        s_wn = _bdot_nt(qg, wn_ref[0, :, gs])
        p_wb, p_wn = two_part_attention(s_wb.reshape(HPG, TQ, wcache), bcast(mask_w),
                                        s_wn.reshape(HPG, TQ, TQ), bcast(mask_new))
        o_win = (_bdot(p_wb.reshape(rows, wcache), wb_ref[0, :, vs_cols])
                 + _bdot(p_wn.reshape(rows, TQ), wn_ref[0, :, vs_cols]))
        for h in range(HPG):
            r8 = slice(h * TQ, (h + 1) * TQ)
            gc = (g // 2) * LANES + ((g % 2) * HPG + h) * 3
            o = (gt[:, gc:gc + 1] * o_cmp[r8] + gt[:, gc + 1:gc + 2] * o_sel[r8]
                 + gt[:, gc + 2:gc + 3] * o_win[r8])
            c0 = (g * HPG + h) * HEAD_DIM
            o_ref[0, :, c0:c0 + HEAD_DIM] = o


def _nsa_sample(pages, layer, page_ids, q, gates, kc, vc, new_rows, win_buf, win_new, past_len, n_new):
    n = q.shape[0]
    wc = win_buf.shape[1]
    n_blk = -(-(past_len + n_new) // SEL_BLOCK)
    page_spec = lambda k: pl.BlockSpec((1, PAGE_SIZE, 2 * KV_COLS),
                                       lambda b, pt: (pt[b * PAGES_PER_SEQ + k], 0, 2 * layer + 1))
    seq = lambda a: pl.BlockSpec((1,) + a.shape[1:], lambda b, pt: (b,) + (0,) * (a.ndim - 1))
    full = lambda a: pl.BlockSpec(a.shape, lambda b, pt: (0,) * a.ndim)
    ov, ex = _overlap_matrix(n_blk), _expand_matrix(past_len)
    per_seq = (q, gates, kc, vc, new_rows, win_buf, win_new)
    grid_spec = pltpu.PrefetchScalarGridSpec(
        num_scalar_prefetch=1,
        grid=(n,),
        in_specs=([page_spec(k) for k in range(PAGES_PER_SEQ)] + [seq(a) for a in per_seq]
                  + [full(ov), full(ex)]),
        out_specs=pl.BlockSpec((1, TQ, N_HEADS * HEAD_DIM), lambda b, pt: (b, 0, 0)),
    )
    return pl.pallas_call(
        functools.partial(_nsa_sample_kernel, past_len=past_len, n_new=n_new),
        grid_spec=grid_spec,
        out_shape=jax.ShapeDtypeStruct((n, TQ, N_HEADS * HEAD_DIM), F32),
        compiler_params=_cparams(("parallel",)),
        name="nsa_sample",
    )(page_ids, *([pages] * PAGES_PER_SEQ), *per_seq, ov, ex)


WKV_SAMPLE_CHUNK = 16
WKV_PROMPT_SEQS = 2
WKV_SAMPLE_SEQS = 8


def _pad_rows(a, n):
    return jnp.pad(a, ((0, 0), (0, n - a.shape[1]), (0, 0)))


def _nsa_layer(x, n_prompt, b, t, db, ds, pages, layer, page_table, win_cache, w_in, w_out, cmp, ln_g,
               ln_b):
    d = x.shape[1]
    qd = N_HEADS * HEAD_DIM
    q = _proj(x, w_in, 512, 0, qd)
    rows = _proj(x, w_in, 512, 2, 4 * KV_COLS)
    win = _proj(x, w_in, 512, 4, 2 * KV_COLS)
    n_gate = 2 * HPG * 3
    g0 = qd + 6 * KV_COLS
    wg = jnp.concatenate(
        [jnp.pad(w_in[:, g0 + p * n_gate:g0 + (p + 1) * n_gate], ((0, 0), (0, LANES - n_gate)))
         for p in range(KV_HEADS // 2)], axis=1)
    gates = _proj(x, wg, wg.shape[1], 0, wg.shape[1])
    past_len = page_table.shape[1] * PAGE_SIZE

    prompt_pages = rows.reshape(rows.shape[0] // PAGE_SIZE, PAGE_SIZE, 4 * KV_COLS)
    kc_p, vc_p = _compress(prompt_pages, 0, jnp.arange(b * PAGES_PER_SEQ, dtype=jnp.int32), b, *cmp)
    o_p = _nsa_prompt(q, gates, kc_p, vc_p, rows, win, b, t)

    q_s, rows_s, win_s, gates_s = (a[n_prompt:].reshape(db, ds, -1) for a in (q, rows, win, gates))
    page_ids = page_table.reshape(-1).astype(jnp.int32)
    kc_s, vc_s = _compress(pages, layer, page_ids, db, *cmp)
    win_buf = win_cache.reshape(db, win_cache.shape[1], 2 * KV_COLS)
    o_s = _nsa_sample(pages, layer, page_ids, _pad_rows(q_s, TQ), _pad_rows(gates_s, TQ), kc_s, vc_s,
                      _pad_rows(rows_s, TQ), win_buf, _pad_rows(win_s, TQ), past_len, ds)[:, :ds]

    o = jnp.concatenate([o_p, o_s.reshape(db * ds, qd)], axis=0)
    x_new = _out_ln(o, None, w_out, x, ln_g, ln_b)
    keep = min(WINDOW, t)
    win_p = win[:n_prompt].reshape(b, t, 2 * KV_COLS)
    outs = (rows[:n_prompt].reshape(b, t, 4, KV_HEADS, HEAD_DIM),
            win_p[:, t - keep:].reshape(b, keep, 2, KV_HEADS, HEAD_DIM),
            rows_s.reshape(db, ds, 4, KV_HEADS, HEAD_DIM),
            jnp.concatenate([win_buf, win_s], axis=1)[:, ds:].reshape(db, -1, 2, KV_HEADS, HEAD_DIM))
    return x_new, outs


def _rwkv_layer(x, n_prompt, b, t, db, ds, shift_s, state_s, mu, w_rkv, w0, w1, w2, a0, a1, a2, g1, g2,
                k_k, k_a, r_k, gn_g, gn_b, w_out, ln_g, ln_b):
    d = x.shape[1]
    xs = x[n_prompt:].reshape(db, ds, d)
    prev_s = jnp.concatenate([shift_s[:, None].astype(x.dtype), xs[:, :-1]], axis=1).reshape(db * ds, d)
    tiles, p_tiles = x.shape[0] // ROW_TILE, n_prompt // ROW_TILE
    last_rows = x[ROW_TILE - 1:n_prompt:ROW_TILE]
    carry = jnp.concatenate([jnp.zeros((1, d), x.dtype), last_rows[:-1]], axis=0)
    seq_start = (jnp.arange(p_tiles) * ROW_TILE) % t == 0
    carry = jnp.where(seq_start[:, None], 0.0, carry)
    carry = jnp.pad(carry, ((0, tiles - p_tiles), (0, 0))).reshape(tiles, 1, d)
    r, k, v, ld, ag, g = _rwkv_pre(x, carry, prev_s, mu, w_rkv, w0, w1, w2, a0, a1, a2, g1, g2)
    streams = (r, k, v, ld, ag)
    head_params = (k_k, k_a, r_k.reshape(d), gn_g, gn_b)
    zero_state = jnp.zeros((b, RWKV_HEADS, RWKV_HEAD, RWKV_HEAD), F32)
    y_p, wkv_p = _wkv(streams, b, t, zero_state, *head_params, WKV_CHUNK, WKV_PROMPT_SEQS)
    streams_s = [_pad_rows(a[n_prompt:].reshape(db, ds, d), WKV_SAMPLE_CHUNK).reshape(-1, d)
                 for a in streams]
    y_s, wkv_s = _wkv(streams_s, db, WKV_SAMPLE_CHUNK, state_s.astype(F32), *head_params,
                      WKV_SAMPLE_CHUNK, WKV_SAMPLE_SEQS)
    y = jnp.concatenate([y_p.reshape(n_prompt, d), y_s[:, :ds].reshape(db * ds, d)], axis=0)
    x_new = _out_ln(y, g, w_out, x, ln_g, ln_b)
    return x_new, (wkv_p, x[t - 1:n_prompt:t], wkv_s.astype(state_s.dtype), xs[:, -1])


def kernel(x_prompt, x_sample, cache_nsa_paged, cache_nsa_win, state_rwkv_wkv, state_rwkv_shift, page_table, nsa_w_in, nsa_w_out, nsa_cmp_pe, nsa_cmp_w1, nsa_cmp_b1, nsa_cmp_w2, nsa_cmp_b2, rwkv_mu, rwkv_w_rkv, rwkv_w0, rwkv_w1, rwkv_w2, rwkv_a0, rwkv_a1, rwkv_a2, rwkv_g1, rwkv_g2, rwkv_k_k, rwkv_k_a, rwkv_r_k, rwkv_gn_g, rwkv_gn_b, rwkv_w_out, ln_g, ln_b, moe_w_router, moe_b_router, moe_w_up, moe_b_up, moe_w_down, moe_b_down):
    b, t, d = x_prompt.shape
    db, ds, _ = x_sample.shape
    n_prompt = b * t
    x = jnp.concatenate([x_prompt.reshape(n_prompt, d), x_sample.reshape(db * ds, d)], axis=0)
    n_phys = cache_nsa_paged.shape[0]
    nsa_outs, rwkv_outs = [], []
    for i in range(DEPTH):
        j = i // 2
        if i % 2 == 0:
            pages = cache_nsa_paged.reshape(n_phys, PAGE_SIZE, -1)
            cmp = (nsa_cmp_pe[j], nsa_cmp_w1[j], nsa_cmp_b1[j], nsa_cmp_w2[j], nsa_cmp_b2[j])
            x, outs = _nsa_layer(x, n_prompt, b, t, db, ds, pages, j, page_table, cache_nsa_win[j],
                                 nsa_w_in[j], nsa_w_out[j], cmp, ln_g[i, 0], ln_b[i, 0])
            nsa_outs.append(outs)
        else:
            x, outs = _rwkv_layer(x, n_prompt, b, t, db, ds, state_rwkv_shift[j], state_rwkv_wkv[j],
                                  rwkv_mu[j], rwkv_w_rkv[j], rwkv_w0[j], rwkv_w1[j], rwkv_w2[j],
                                  rwkv_a0[j], rwkv_a1[j], rwkv_a2[j], rwkv_g1[j], rwkv_g2[j],
                                  rwkv_k_k[j], rwkv_k_a[j], rwkv_r_k[j], rwkv_gn_g[j], rwkv_gn_b[j],
                                  rwkv_w_out[j], ln_g[i, 0], ln_b[i, 0])
            rwkv_outs.append(outs)
        x = _moe_layer(x, moe_w_router[i], moe_b_router[i], moe_w_up[i], moe_b_up[i], moe_w_down[i],
                       moe_b_down[i], ln_g[i, 1], ln_b[i, 1])
    p_rows, p_win, s_rows, s_win = (jnp.stack(z, axis=k) for z, k in zip(zip(*nsa_outs), (2, 0, 2, 0)))
    p_wkv, p_shift, s_wkv, s_shift = (jnp.stack(z, axis=0) for z in zip(*rwkv_outs))
    return (x[:n_prompt].reshape(b, t, d), x[n_prompt:].reshape(db, ds, d),
            p_rows, p_win, p_wkv, p_shift, s_rows, s_win, s_wkv, s_shift)
```
